```python
import math
import jax, jax.numpy as jnp
from jax import lax
import numpy as np

D_MODEL = 1024
BATCH = 8
SEQ = 4096
DEPTH = 2

PLE_DIM = 256
MIX_W = D_MODEL
N_MIXERS = 4
GROUP_W = MIX_W // N_MIXERS
EPS = 1e-6

S5_W = GROUP_W
S5_CH = 16
S5_GROUPS = S5_W // S5_CH
S5_STATE = 64
S5_DT_MIN = 1e-3
S5_DT_MAX = 1e-1

ATT_W = GROUP_W
ATT_HEAD_DIM = 64
ATT_HEADS = ATT_W // ATT_HEAD_DIM
MOBA_BLOCK = 256
MOBA_TOPK = 3
MOBA_Q_CHUNK = 32

SCONV_W = GROUP_W
SCONV_K = 3

SSD_W = GROUP_W
SSD_HEAD_DIM = 64
SSD_HEADS = SSD_W // SSD_HEAD_DIM
SSD_GROUPS = 2
SSD_STATE = 128
SSD_CONV_K = 4
SSD_CHUNK = 128
SSD_CONV_CH = SSD_W + 2 * SSD_GROUPS * SSD_STATE

D_FF = 2816
FFN_CONV_K = 3

PROJ_SIZES = (S5_W, ATT_W, ATT_W, ATT_W, SCONV_W, SCONV_W, SCONV_W, SSD_W, SSD_CONV_CH, SSD_HEADS)
PROJ_OUT = sum(PROJ_SIZES)

kernel_name = "hybrid_s5_moba_shortconv_ssd_block"


def rmsnorm(x, g):
    xf = x.astype(jnp.float32)
    y = xf * lax.rsqrt(jnp.mean(xf * xf, axis=-1, keepdims=True) + EPS)
    return (y * g.astype(jnp.float32)).astype(x.dtype)


def group_rmsnorm(x, g, n_groups):
    lead = x.shape[:-1]
    xf = x.astype(jnp.float32).reshape(lead + (n_groups, x.shape[-1] // n_groups))
    y = xf * lax.rsqrt(jnp.mean(xf * xf, axis=-1, keepdims=True) + EPS)
    return (y.reshape(x.shape) * g.astype(jnp.float32)).astype(x.dtype)


def causal_dwconv(x, w):
    k, c = w.shape
    return lax.conv_general_dilated(
        x, w[:, None, :].astype(x.dtype), window_strides=(1,), padding=[(k - 1, 0)],
        dimension_numbers=("NWC", "WIO", "NWC"), feature_group_count=c)


def _s5_combine(e1, e2):
    a1r, a1i, b1r, b1i = e1
    a2r, a2i, b2r, b2i = e2
    return (a2r * a1r - a2i * a1i,
            a2r * a1i + a2i * a1r,
            a2r * b1r - a2i * b1i + b2r,
            a2r * b1i + a2i * b1r + b2i)


def s5_mixer(u, a_re, a_im, log_dt, b_re, b_im, c_re, c_im, d_skip, w_glu):
    f32 = jnp.float32
    bsz, t, _ = u.shape
    uf = u.astype(f32)
    ug = uf.reshape(bsz, t, S5_GROUPS, S5_CH)
    lr, li = a_re.astype(f32), a_im.astype(f32)
    dt = jnp.exp(log_dt.astype(f32))[:, None]
    mag = jnp.exp(lr * dt)
    ab_r, ab_i = mag * jnp.cos(li * dt), mag * jnp.sin(li * dt)
    den = lr * lr + li * li
    nr = ab_r - 1.0
    coef_r = (nr * lr + ab_i * li) / den
    coef_i = (ab_i * lr - nr * li) / den
    br, bi = b_re.astype(f32), b_im.astype(f32)
    bb_r = coef_r[..., None] * br - coef_i[..., None] * bi
    bb_i = coef_r[..., None] * bi + coef_i[..., None] * br
    bu_r = jnp.einsum("btgc,gpc->btgp", ug, bb_r)
    bu_i = jnp.einsum("btgc,gpc->btgp", ug, bb_i)
    a_r = jnp.broadcast_to(ab_r[None, None], (1, t) + ab_r.shape)
    a_i = jnp.broadcast_to(ab_i[None, None], (1, t) + ab_i.shape)
    _, _, h_r, h_i = lax.associative_scan(_s5_combine, (a_r, a_i, bu_r, bu_i), axis=1)
    y = (jnp.einsum("btgp,gcp->btgc", h_r, c_re.astype(f32))
         - jnp.einsum("btgp,gcp->btgc", h_i, c_im.astype(f32)))
    y = y.reshape(bsz, t, S5_W) + d_skip.astype(f32) * uf
    g = jax.nn.gelu(y)
    out = g * jax.nn.sigmoid(g @ w_glu.astype(f32))
    return out.astype(u.dtype)


def moba_attention(q, k, v, q_g, k_g):
    f32 = jnp.float32
    bsz, t, _ = q.shape
    h, dh = ATT_HEADS, ATT_HEAD_DIM
    q = rmsnorm(q.reshape(bsz, t, h, dh), q_g).transpose(0, 2, 1, 3)
    k = rmsnorm(k.reshape(bsz, t, h, dh), k_g).transpose(0, 2, 1, 3)
    v = v.reshape(bsz, t, h, dh).transpose(0, 2, 1, 3)
    nb = -(-t // MOBA_BLOCK)
    pad = ((0, 0), (0, 0), (0, nb * MOBA_BLOCK - t), (0, 0))
    kb = jnp.pad(k, pad).reshape(bsz, h, nb, MOBA_BLOCK, dh)
    vb = jnp.pad(v, pad).reshape(bsz, h, nb, MOBA_BLOCK, dh)
    k_mean = jnp.mean(kb.astype(f32), axis=3)
    k_sel = min(MOBA_TOPK, nb)
    scale = dh ** -0.5
    qc_len = MOBA_Q_CHUNK
    n_chunks = t // qc_len
    q_chunks = q.reshape(bsz, h, n_chunks, qc_len, dh).transpose(2, 0, 1, 3, 4)
    starts = jnp.arange(n_chunks, dtype=jnp.int32) * qc_len
    b_ix = jnp.arange(bsz)[:, None, None, None]
    h_ix = jnp.arange(h)[None, :, None, None]
    blk_ids = jnp.arange(nb, dtype=jnp.int32)
    offs = jnp.arange(MOBA_BLOCK, dtype=jnp.int32)
    is_past_slot = jnp.arange(k_sel + 1) < k_sel

    def attend_chunk(args):
        qc, start = args
        qpos = start + jnp.arange(qc_len, dtype=jnp.int32)
        qblk = qpos // MOBA_BLOCK
        gate = jnp.einsum("bhqd,bhnd->bhqn", qc.astype(f32), k_mean)
        gate = jnp.where(blk_ids[None, :] < qblk[:, None], gate, -jnp.inf)
        _, top_idx = lax.top_k(gate, k_sel)
        own = jnp.broadcast_to(qblk[None, None, :, None], (bsz, h, qc_len, 1)).astype(top_idx.dtype)
        sel = jnp.concatenate([top_idx, own], axis=-1)
        kg = kb[b_ix, h_ix, sel]
        vg = vb[b_ix, h_ix, sel]
        logits = jnp.einsum("bhqd,bhqnsd->bhqns", qc, kg).astype(f32) * scale
        kpos = sel[..., None] * MOBA_BLOCK + offs
        valid = jnp.where(is_past_slot[:, None],
                          (sel < qblk[:, None])[..., None],
                          kpos <= qpos[:, None, None])
        logits = jnp.where(valid, logits, -jnp.inf)
        probs = jax.nn.softmax(logits.reshape(bsz, h, qc_len, -1), axis=-1).reshape(logits.shape)
        return jnp.einsum("bhqns,bhqnsd->bhqd", probs.astype(vg.dtype), vg)

    out = lax.map(attend_chunk, (q_chunks, starts))
    return out.transpose(1, 0, 3, 2, 4).reshape(bsz, t, ATT_W)


def short_conv_mixer(wx, gate_b, gate_c, w):
    return gate_b * causal_dwconv(gate_c * wx, w)


def ssd_chunked(x, dt, a, bm, cm):
    bsz, t, h, p = x.shape
    g, n = bm.shape[2], bm.shape[3]
    lc = SSD_CHUNK
    c = t // lc
    rep = h // g
    xd = (x * dt[..., None]).reshape(bsz, c, lc, h, p)
    adt = (dt * a).reshape(bsz, c, lc, h).transpose(0, 3, 1, 2)
    bh = jnp.repeat(bm.reshape(bsz, c, lc, g, n), rep, axis=3)
    ch = jnp.repeat(cm.reshape(bsz, c, lc, g, n), rep, axis=3)
    a_cs = jnp.cumsum(adt, axis=-1)
    seg = a_cs[..., :, None] - a_cs[..., None, :]
    causal = jnp.tril(jnp.ones((lc, lc), dtype=bool))
    decay_in = jnp.where(causal, jnp.exp(jnp.where(causal, seg, 0.0)), 0.0)
    scores = jnp.einsum("bclhn,bcshn->bhcls", ch, bh) * decay_in
    y_diag = jnp.einsum("bhcls,bcshp->bclhp", scores, xd)
    decay_to_end = jnp.exp(a_cs[..., -1:] - a_cs)
    states = jnp.einsum("bclhn,bclhp->bchpn",
                        bh * decay_to_end.transpose(0, 2, 3, 1)[..., None], xd)
    chunk_decay = jnp.exp(a_cs[..., -1])

    def step(state, inp):
        s, d = inp
        return state * d[..., None, None] + s, state

    _, h_prev = lax.scan(step, jnp.zeros((bsz, h, p, n), jnp.float32),
                         (states.transpose(1, 0, 2, 3, 4), chunk_decay.transpose(2, 0, 1)))
    h_prev = h_prev.transpose(1, 0, 2, 3, 4)
    y_off = (jnp.einsum("bclhn,bchpn->bclhp", ch, h_prev)
             * jnp.exp(a_cs).transpose(0, 2, 3, 1)[..., None])
    return (y_diag + y_off).reshape(bsz, t, h, p)


def ssd_mixer(z, xbc, dt_raw, conv_w, conv_b, dt_bias, a_log, d_skip):
    f32 = jnp.float32
    bsz, t, _ = z.shape
    xbc = jax.nn.silu(causal_dwconv(xbc, conv_w) + conv_b.astype(xbc.dtype))
    gn = SSD_GROUPS * SSD_STATE
    xs, bm, cm = jnp.split(xbc, [SSD_W, SSD_W + gn], axis=-1)
    xs = xs.astype(f32).reshape(bsz, t, SSD_HEADS, SSD_HEAD_DIM)
    bm = bm.astype(f32).reshape(bsz, t, SSD_GROUPS, SSD_STATE)
    cm = cm.astype(f32).reshape(bsz, t, SSD_GROUPS, SSD_STATE)
    dt = jax.nn.softplus(dt_raw.astype(f32) + dt_bias.astype(f32))
    a = -jnp.exp(a_log.astype(f32))
    y = ssd_chunked(xs, dt, a, bm, cm) + d_skip.astype(f32)[:, None] * xs
    y = y.reshape(bsz, t, SSD_W) * jax.nn.silu(z.astype(f32))
    return y.astype(z.dtype)


def conv_ffn(v, w_gate, w_up, conv_w, w_down):
    g = causal_dwconv(v @ w_gate, conv_w)
    return (jax.nn.silu(g) * (v @ w_up)) @ w_down


def setup_inputs(seed: int = 0) -> dict:
    key = jax.random.key(seed)
    ks = jax.random.split(key, 40)
    f32 = jnp.float32
    L = DEPTH

    def nrm(k, shape, scale):
        return jax.random.normal(k, shape, f32) * scale

    def gain(k, n):
        return 1.0 + 0.02 * jax.random.normal(k, (L, n), f32)

    s5_log_dt = jax.random.uniform(ks[5], (L, S5_GROUPS), f32,
                                   math.log(S5_DT_MIN), math.log(S5_DT_MAX))
    ssd_dt0 = jnp.exp(jax.random.uniform(ks[16], (L, SSD_HEADS), f32, math.log(1e-3), math.log(1e-1)))
    ssd_dt_bias = ssd_dt0 + jnp.log(-jnp.expm1(-ssd_dt0))
    ssd_a_log = jnp.log(jax.random.uniform(ks[17], (L, SSD_HEADS), f32, 1.0, 16.0))
    n_idx = jnp.arange(S5_STATE, dtype=f32)
    return {
        "x": nrm(ks[0], (BATCH, SEQ, D_MODEL), 1.0),
        "p": nrm(ks[1], (L, BATCH, SEQ, PLE_DIM), 1.0),
        "mix_norm_g": gain(ks[2], D_MODEL),
        "w_in": nrm(ks[3], (L, D_MODEL, PROJ_OUT), D_MODEL ** -0.5),
        "s5_a_re": -0.5 + 0.01 * jax.random.normal(ks[4], (L, S5_GROUPS, S5_STATE), f32),
        "s5_a_im": math.pi * n_idx + 0.01 * jax.random.normal(ks[6], (L, S5_GROUPS, S5_STATE), f32),
        "s5_log_dt": s5_log_dt,
        "s5_b_re": nrm(ks[7], (L, S5_GROUPS, S5_STATE, S5_CH), (2 * S5_CH) ** -0.5),
        "s5_b_im": nrm(ks[8], (L, S5_GROUPS, S5_STATE, S5_CH), (2 * S5_CH) ** -0.5),
        "s5_c_re": nrm(ks[9], (L, S5_GROUPS, S5_CH, S5_STATE), 0.5),
        "s5_c_im": nrm(ks[10], (L, S5_GROUPS, S5_CH, S5_STATE), 0.5),
        "s5_d": nrm(ks[11], (L, S5_W), 1.0),
        "s5_w_glu": nrm(ks[12], (L, S5_W, S5_W), S5_W ** -0.5),
        "moba_q_g": gain(ks[13], ATT_HEAD_DIM),
        "moba_k_g": gain(ks[14], ATT_HEAD_DIM),
        "sconv_w": nrm(ks[15], (L, SCONV_K, SCONV_W), SCONV_K ** -0.5),
        "ssd_conv_w": nrm(ks[18], (L, SSD_CONV_K, SSD_CONV_CH), SSD_CONV_K ** -0.5),
        "ssd_conv_b": nrm(ks[19], (L, SSD_CONV_CH), 0.01),
        "ssd_dt_bias": ssd_dt_bias,
        "ssd_a_log": ssd_a_log,
        "ssd_d": gain(ks[20], SSD_HEADS),
        "merge_norm_g": gain(ks[21], MIX_W),
        "w_out": nrm(ks[22], (L, MIX_W, D_MODEL), MIX_W ** -0.5),
        "ffn_norm_g": gain(ks[23], D_MODEL),
        "ffn_w_gate": nrm(ks[24], (L, D_MODEL, D_FF), D_MODEL ** -0.5),
        "ffn_w_up": nrm(ks[25], (L, D_MODEL, D_FF), D_MODEL ** -0.5),
        "ffn_conv_w": nrm(ks[26], (L, FFN_CONV_K, D_FF), FFN_CONV_K ** -0.5),
        "ffn_w_down": nrm(ks[27], (L, D_FF, D_MODEL), D_FF ** -0.5),
        "ple_norm_g": gain(ks[28], D_MODEL),
        "ple_w_gate": nrm(ks[29], (L, D_MODEL, D_MODEL), D_MODEL ** -0.5),
        "ple_w_proj": nrm(ks[30], (L, PLE_DIM, D_MODEL), PLE_DIM ** -0.5),
    }


def reference(x, p, mix_norm_g, w_in, s5_a_re, s5_a_im, s5_log_dt, s5_b_re, s5_b_im,
              s5_c_re, s5_c_im, s5_d, s5_w_glu, moba_q_g, moba_k_g, sconv_w,
              ssd_conv_w, ssd_conv_b, ssd_dt_bias, ssd_a_log, ssd_d, merge_norm_g, w_out,
              ffn_norm_g, ffn_w_gate, ffn_w_up, ffn_conv_w, ffn_w_down,
              ple_norm_g, ple_w_gate, ple_w_proj):
    split_points = [int(s) for s in np.cumsum(PROJ_SIZES)[:-1]]
    h = x
    for i in range(DEPTH):
        u = rmsnorm(h, mix_norm_g[i])
        proj = u @ w_in[i]
        (s5_u, q, k, v, sc_x, sc_b, sc_c, ssd_z, ssd_xbc, ssd_dt) = jnp.split(proj, split_points, axis=-1)
        y_a = s5_mixer(s5_u, s5_a_re[i], s5_a_im[i], s5_log_dt[i], s5_b_re[i], s5_b_im[i],
                       s5_c_re[i], s5_c_im[i], s5_d[i], s5_w_glu[i])
        y_b = moba_attention(q, k, v, moba_q_g[i], moba_k_g[i])
        y_c = short_conv_mixer(sc_x, sc_b, sc_c, sconv_w[i])
        y_d = ssd_mixer(ssd_z, ssd_xbc, ssd_dt, ssd_conv_w[i], ssd_conv_b[i],
                        ssd_dt_bias[i], ssd_a_log[i], ssd_d[i])
        mixed = group_rmsnorm(jnp.concatenate([y_a, y_b, y_c, y_d], axis=-1), merge_norm_g[i], N_MIXERS)
        h = h + mixed @ w_out[i]
        h = h + conv_ffn(rmsnorm(h, ffn_norm_g[i]), ffn_w_gate[i], ffn_w_up[i], ffn_conv_w[i], ffn_w_down[i])
        gate = jax.nn.sigmoid(rmsnorm(h, ple_norm_g[i]) @ ple_w_gate[i])
        h = h + gate * (p[i] @ ple_w_proj[i])
    return h
```

```python
import functools
import math

import jax
import jax.numpy as jnp
from jax import lax
from jax.experimental import pallas as pl
from jax.experimental.pallas import tpu as pltpu

F32 = jnp.float32
BF16 = jnp.bfloat16
EPS = 1e-6
NEG = -1e30
HIGHEST = lax.Precision.HIGHEST

D_MODEL = 1024
PLE_DIM = 256
GROUP_W = 256
N_MIXERS = 4
S5_CH = 16
S5_GROUPS = 16
S5_STATE = 64
S5_L = 16
HEAD_DIM = 64
N_HEADS = 4
MOBA_BLOCK = 256
MOBA_TOPK = 3
SSD_GROUPS = 2
SSD_STATE = 128
SSD_CONV_K = 4
SSD_CHUNK = 128
SSD_CONV_CH = GROUP_W + 2 * SSD_GROUPS * SSD_STATE
SCONV_K = 3
D_FF = 2816
FFN_CONV_K = 3
HALO = 8
LANE = 128
PROJ_PAD = 2944
VMEM_LIMIT = 56 * 1024 * 1024


def _cparams(sem):
    return pltpu.CompilerParams(dimension_semantics=sem, vmem_limit_bytes=VMEM_LIMIT)


def _rms(x, g):
    return x * lax.rsqrt(jnp.mean(x * x, axis=-1, keepdims=True) + EPS) * g


def _sigmoid(x):
    return 1.0 / (1.0 + jnp.exp(-x))


def _silu(x):
    return x * _sigmoid(x)


def _dot(a, b):
    return jnp.dot(a, b, preferred_element_type=F32)


def _dot_nt(a, b, precision=None):
    return lax.dot_general(a, b, (((1,), (1,)), ((), ())), preferred_element_type=F32,
                           precision=precision)


def _inproj_kernel(h_ref, g_ref, w_ref, s5u_ref, qkv_ref, sc_ref, z_ref, xbc_ref, dt_ref):
    u = _rms(h_ref[...], g_ref[...]).astype(BF16)
    s5u_ref[...] = _dot(u, w_ref[:, 0:256])
    qkv_ref[...] = _dot(u, w_ref[:, 256:1024])
    sc_ref[...] = _dot(u, w_ref[:, 1024:1792])
    z_ref[...] = _dot(u, w_ref[:, 1792:2048])
    xbc_ref[...] = _dot(u, w_ref[:, 2048:2816])
    dt_ref[...] = _dot(u, w_ref[:, 2816:PROJ_PAD])


def _inproj(h, g, w, tm=512):
    n = h.shape[0]
    widths = (256, 768, 768, 256, 768, LANE)
    return pl.pallas_call(
        _inproj_kernel,
        grid=(n // tm,),
        in_specs=[pl.BlockSpec((tm, D_MODEL), lambda i: (i, 0)),
                  pl.BlockSpec((1, D_MODEL), lambda i: (0, 0)),
                  pl.BlockSpec((D_MODEL, PROJ_PAD), lambda i: (0, 0))],
        out_specs=[pl.BlockSpec((tm, wd), lambda i: (i, 0)) for wd in widths],
        out_shape=[jax.ShapeDtypeStruct((n, wd), F32) for wd in widths],
        compiler_params=_cparams(("parallel",)),
        name="inproj",
    )(h, g, w)


def _s5_param_mats(a_re, a_im, log_dt, b_re, b_im, c_re, c_im):
    L = S5_L
    dt = jnp.exp(log_dt)[:, None]
    taus = jnp.arange(L + 1, dtype=F32)[None, :, None]
    mag = jnp.exp((a_re * dt)[:, None, :] * taus)
    ang = (a_im * dt)[:, None, :] * taus
    pw_r, pw_i = mag * jnp.cos(ang), mag * jnp.sin(ang)
    ab_r, ab_i = pw_r[:, 1], pw_i[:, 1]
    den = a_re * a_re + a_im * a_im
    nr = ab_r - 1.0
    coef_r = (nr * a_re + ab_i * a_im) / den
    coef_i = (ab_i * a_re - nr * a_im) / den
    bb_r = coef_r[..., None] * b_re - coef_i[..., None] * b_im
    bb_i = coef_r[..., None] * b_im + coef_i[..., None] * b_re
    rev_r, rev_i = pw_r[:, L - 1::-1], pw_i[:, L - 1::-1]
    m_r = jnp.einsum("gsp,gpc->gscp", rev_r, bb_r) - jnp.einsum("gsp,gpc->gscp", rev_i, bb_i)
    m_i = jnp.einsum("gsp,gpc->gscp", rev_r, bb_i) + jnp.einsum("gsp,gpc->gscp", rev_i, bb_r)
    m_r = m_r.reshape(S5_GROUPS, L * S5_CH, S5_STATE)
    m_i = m_i.reshape(S5_GROUPS, L * S5_CH, S5_STATE)
    cp_r = c_re[:, None] * pw_r[:, :, None, :] - c_im[:, None] * pw_i[:, :, None, :]
    cp_i = c_re[:, None] * pw_i[:, :, None, :] + c_im[:, None] * pw_r[:, :, None, :]
    kk = (jnp.einsum("gtcp,gpd->gtcd", cp_r[:, :L], bb_r, precision=HIGHEST)
          - jnp.einsum("gtcp,gpd->gtcd", cp_i[:, :L], bb_i, precision=HIGHEST))
    s_ix = jnp.arange(L)[:, None]
    l_ix = jnp.arange(L)[None, :]
    lag = l_ix - s_ix
    t_full = jnp.where((lag >= 0)[None, :, :, None, None], kk[:, jnp.clip(lag, 0, L - 1)], 0.0)
    t_mat = t_full.transpose(0, 1, 4, 2, 3).reshape(S5_GROUPS, L * S5_CH, L * S5_CH)
    p_r = cp_r[:, 1:].transpose(0, 3, 1, 2).reshape(S5_GROUPS, S5_STATE, L * S5_CH)
    p_i = (-cp_i[:, 1:]).transpose(0, 3, 1, 2).reshape(S5_GROUPS, S5_STATE, L * S5_CH)
    al_r = pw_r[:, L][:, None, :]
    al_i = pw_i[:, L][:, None, :]
    return (t_mat.astype(BF16), m_r.astype(BF16), m_i.astype(BF16), p_r.astype(BF16),
            p_i.astype(BF16), al_r, al_i)


def _s5_kernel(u_ref, t_ref, mr_ref, mi_ref, pr_ref, pi_ref, alr_ref, ali_ref, y_ref,
               locr, loci, spr, spi, *, n_chunks, bsz):
    u = u_ref[0]
    locr[...] = _dot(u, mr_ref[0])
    loci[...] = _dot(u, mi_ref[0])
    ar = jnp.broadcast_to(alr_ref[0], (bsz, S5_STATE))
    ai = jnp.broadcast_to(ali_ref[0], (bsz, S5_STATE))

    def step(k, carry):
        sr, si = carry
        r0 = pl.multiple_of(k * bsz, bsz)
        spr[pl.ds(r0, bsz), :] = sr
        spi[pl.ds(r0, bsz), :] = si
        lr = locr[pl.ds(r0, bsz), :]
        li = loci[pl.ds(r0, bsz), :]
        return ar * sr - ai * si + lr, ar * si + ai * sr + li

    zero = jnp.zeros((bsz, S5_STATE), F32)
    lax.fori_loop(0, n_chunks, step, (zero, zero))
    y = _dot(u, t_ref[0])
    y = y + _dot(spr[...].astype(BF16), pr_ref[0])
    y = y + _dot(spi[...].astype(BF16), pi_ref[0])
    y_ref[0] = y


def _s5_scan(s5u, mats, bsz, seq):
    n_chunks = seq // S5_L
    rows = n_chunks * bsz
    lc = S5_L * S5_CH
    u = s5u.astype(BF16).reshape(bsz, n_chunks, S5_L, S5_GROUPS, S5_CH)
    u = u.transpose(3, 1, 0, 2, 4).reshape(S5_GROUPS, rows, lc)
    t_mat, m_r, m_i, p_r, p_i, al_r, al_i = mats

    def gspec(a, b):
        return pl.BlockSpec((1, a, b), lambda g: (g, 0, 0))

    y = pl.pallas_call(
        functools.partial(_s5_kernel, n_chunks=n_chunks, bsz=bsz),
        grid=(S5_GROUPS,),
        in_specs=[gspec(rows, lc), gspec(lc, lc), gspec(lc, S5_STATE), gspec(lc, S5_STATE),
                  gspec(S5_STATE, lc), gspec(S5_STATE, lc), gspec(1, S5_STATE), gspec(1, S5_STATE)],
        out_specs=gspec(rows, lc),
        out_shape=jax.ShapeDtypeStruct((S5_GROUPS, rows, lc), F32),
        scratch_shapes=[pltpu.VMEM((rows, S5_STATE), F32)] * 4,
        compiler_params=_cparams(("parallel",)),
        name="s5_scan",
    )(u, t_mat, m_r, m_i, p_r, p_i, al_r, al_i)
    y = y.reshape(S5_GROUPS, n_chunks, bsz, S5_L, S5_CH).transpose(2, 1, 3, 0, 4)
    return y.reshape(bsz * seq, GROUP_W)


def _moba_kernel(q_ref, k_ref, v_ref, qg_ref, kg_ref, o_ref, kn, vb, kmean, *, n_blocks):
    i = pl.program_id(1)
    blk = MOBA_BLOCK

    @pl.when(i == 0)
    def _():
        kg = kg_ref[...]
        for h in range(N_HEADS):
            kh = _rms(k_ref[:, h * HEAD_DIM:(h + 1) * HEAD_DIM], kg)
            kn[h] = kh.astype(BF16)
            kmean[h] = jnp.mean(kh.reshape(n_blocks, blk, HEAD_DIM), axis=1)
            vb[h] = v_ref[:, h * HEAD_DIM:(h + 1) * HEAD_DIM].astype(BF16)

    qg = qg_ref[...]
    row = lax.broadcasted_iota(jnp.int32, (blk, blk), 0)
    col = lax.broadcasted_iota(jnp.int32, (blk, blk), 1)
    causal = col <= row
    bid = lax.broadcasted_iota(jnp.int32, (blk, n_blocks), 1)
    past = bid < i
    onehot_row = lax.broadcasted_iota(jnp.int32, (n_blocks, blk), 0)
    own0 = pl.multiple_of(i * blk, blk)

    for h in range(N_HEADS):
        qh = _rms(q_ref[:, h * HEAD_DIM:(h + 1) * HEAD_DIM], qg)
        gate = _dot_nt(qh, kmean[h], precision=HIGHEST)
        gate = jnp.where(past, gate, -jnp.inf)
        cnt = jnp.zeros((blk, n_blocks), F32)
        for m in range(n_blocks):
            gm = gate[:, m:m + 1]
            beats = jnp.where(gm > gate, 1.0, jnp.where((gm == gate) & (bid > m), 1.0, 0.0))
            cnt = cnt + beats
        sel = jnp.where(past & (cnt < MOBA_TOPK), 1.0, 0.0).astype(BF16)

        qb = (qh * (HEAD_DIM ** -0.5)).astype(BF16)
        s = _dot_nt(qb, kn[h, pl.ds(own0, blk), :])
        s = jnp.where(causal, s, NEG)
        m0 = jnp.max(s, axis=-1, keepdims=True)
        p = jnp.exp(s - m0)
        l0 = jnp.sum(p, axis=-1, keepdims=True)
        acc0 = _dot(p.astype(BF16), vb[h, pl.ds(own0, blk), :])

        def body(n, carry, qb=qb, sel=sel, h=h):
            m_run, l_run, acc = carry
            k0 = pl.multiple_of(n * blk, blk)
            onehot = jnp.where(onehot_row == n, 1.0, 0.0).astype(BF16)
            keep = _dot(sel, onehot)
            s = _dot_nt(qb, kn[h, pl.ds(k0, blk), :])
            s = jnp.where(keep > 0.5, s, NEG)
            m_new = jnp.maximum(m_run, jnp.max(s, axis=-1, keepdims=True))
            alpha = jnp.exp(m_run - m_new)
            p = jnp.exp(s - m_new)
            l_new = alpha * l_run + jnp.sum(p, axis=-1, keepdims=True)
            acc = alpha * acc + _dot(p.astype(BF16), vb[h, pl.ds(k0, blk), :])
            return m_new, l_new, acc

        _, l_fin, acc = lax.fori_loop(0, i, body, (m0, l0, acc0))
        o_ref[:, h * HEAD_DIM:(h + 1) * HEAD_DIM] = acc / l_fin


def _moba(qkv, q_g, k_g, bsz, seq):
    n_blocks = seq // MOBA_BLOCK
    n = bsz * seq
    return pl.pallas_call(
        functools.partial(_moba_kernel, n_blocks=n_blocks),
        grid=(bsz, n_blocks),
        in_specs=[pl.BlockSpec((MOBA_BLOCK, GROUP_W), lambda b, i: (b * n_blocks + i, 0)),
                  pl.BlockSpec((seq, GROUP_W), lambda b, i: (b, 1)),
                  pl.BlockSpec((seq, GROUP_W), lambda b, i: (b, 2)),
                  pl.BlockSpec((1, HEAD_DIM), lambda b, i: (0, 0)),
                  pl.BlockSpec((1, HEAD_DIM), lambda b, i: (0, 0))],
        out_specs=pl.BlockSpec((MOBA_BLOCK, GROUP_W), lambda b, i: (b * n_blocks + i, 0)),
        out_shape=jax.ShapeDtypeStruct((n, GROUP_W), F32),
        scratch_shapes=[pltpu.VMEM((N_HEADS, seq, HEAD_DIM), BF16),
                        pltpu.VMEM((N_HEADS, seq, HEAD_DIM), BF16),
                        pltpu.VMEM((N_HEADS, n_blocks, HEAD_DIM), F32)],
        compiler_params=_cparams(("parallel", "arbitrary")),
        name="moba",
    )(qkv, qkv, qkv, q_g, k_g)


def _ssd_kernel(z_ref, xbc_ref, halo_ref, dt_ref, cw_ref, cb_ref, dtb_ref, a_ref, d_ref, o_ref, state):
    c = pl.program_id(1)
    lc = SSD_CHUNK
    gn = SSD_STATE

    @pl.when(c == 0)
    def _():
        state[...] = jnp.zeros_like(state)

    halo = jnp.where(c == 0, 0.0, halo_ref[...])
    ext = jnp.concatenate([halo, xbc_ref[...]], axis=0)
    conv = cb_ref[...]
    for k in range(SSD_CONV_K):
        off = HALO - (SSD_CONV_K - 1) + k
        conv = conv + cw_ref[k:k + 1, :] * ext[off:off + lc, :]
    xbc = _silu(conv)
    xs = xbc[:, :GROUP_W]
    dtr = dt_ref[...] + dtb_ref[...]
    dt = jnp.maximum(dtr, 0.0) + jnp.log(1.0 + jnp.exp(-jnp.abs(dtr)))
    adt = dt * a_ref[...]
    row = lax.broadcasted_iota(jnp.int32, (lc, lc), 0)
    col = lax.broadcasted_iota(jnp.int32, (lc, lc), 1)
    causal = col <= row
    tril = jnp.where(causal, 1.0, 0.0)
    cs = jnp.dot(tril, adt, preferred_element_type=F32, precision=HIGHEST)
    cs_t = cs.T
    z = z_ref[...]
    dd = d_ref[...]
    for h in range(N_HEADS):
        g = h // (N_HEADS // SSD_GROUPS)
        bm = xbc[:, GROUP_W + g * gn:GROUP_W + (g + 1) * gn]
        cm = xbc[:, GROUP_W + SSD_GROUPS * gn + g * gn:GROUP_W + SSD_GROUPS * gn + (g + 1) * gn]
        cmb = cm.astype(BF16)
        cs_col = cs[:, h:h + 1]
        cs_row = cs_t[h:h + 1, :]
        seg = cs_col - cs_row
        decay = jnp.where(causal, jnp.exp(jnp.where(causal, seg, 0.0)), 0.0)
        scores = (_dot_nt(cmb, bm.astype(BF16)) * decay).astype(BF16)
        hs = slice(h * HEAD_DIM, (h + 1) * HEAD_DIM)
        x_h = xs[:, hs]
        xd = (x_h * dt[:, h:h + 1]).astype(BF16)
        y = _dot(scores, xd)
        s_prev = state[h]
        y = y + _dot(cmb, s_prev.astype(BF16)) * jnp.exp(cs_col)
        cs_last = cs[lc - 1:lc, h:h + 1]
        bd = bm * jnp.exp(cs_last - cs_col)
        state[h] = jnp.exp(cs_last) * s_prev + _dot(bd.T.astype(BF16), xd)
        y = y + dd[:, hs] * x_h
        o_ref[:, hs] = y * _silu(z[:, hs])


def _ssd(z, xbc, dt, conv_w, conv_b, dt_bias, a_log, d_skip, bsz, seq):
    n = bsz * seq
    nc = seq // SSD_CHUNK
    per = SSD_CHUNK // HALO
    dtb = jnp.zeros((1, LANE), F32).at[0, :N_HEADS].set(dt_bias)
    a = jnp.zeros((1, LANE), F32).at[0, :N_HEADS].set(-jnp.exp(a_log))
    dfull = jnp.repeat(d_skip, HEAD_DIM)[None, :]

    def const(shape):
        return pl.BlockSpec(shape, lambda b, c: (0, 0))

    return pl.pallas_call(
        _ssd_kernel,
        grid=(bsz, nc),
        in_specs=[pl.BlockSpec((SSD_CHUNK, GROUP_W), lambda b, c: (b * nc + c, 0)),
                  pl.BlockSpec((SSD_CHUNK, SSD_CONV_CH), lambda b, c: (b * nc + c, 0)),
                  pl.BlockSpec((HALO, SSD_CONV_CH), lambda b, c: (jnp.maximum((b * nc + c) * per - 1, 0), 0)),
                  pl.BlockSpec((SSD_CHUNK, LANE), lambda b, c: (b * nc + c, 0)),
                  const((SSD_CONV_K, SSD_CONV_CH)), const((1, SSD_CONV_CH)), const((1, LANE)),
                  const((1, LANE)), const((1, GROUP_W))],
        out_specs=pl.BlockSpec((SSD_CHUNK, GROUP_W), lambda b, c: (b * nc + c, 0)),
        out_shape=jax.ShapeDtypeStruct((n, GROUP_W), F32),
        scratch_shapes=[pltpu.VMEM((N_HEADS, SSD_STATE, HEAD_DIM), F32)],
        compiler_params=_cparams(("parallel", "arbitrary")),
        name="ssd",
    )(z, xbc, xbc, dt, conv_w, conv_b[None, :], dtb, a, dfull)


def _gelu_tanh(x):
    return 0.5 * x * (1.0 + jnp.tanh(math.sqrt(2.0 / math.pi) * (x + 0.044715 * (x * x * x))))


def _group_norm(y, g):
    return (y * lax.rsqrt(jnp.mean(y * y, axis=-1, keepdims=True) + EPS) * g).astype(BF16)


def _merge_kernel(h_ref, ys5_ref, s5u_ref, att_ref, sc_ref, sch_ref, ssd_ref, s5d_ref, wglu_ref,
                  scw_ref, mg_ref, wo_ref, o_ref, *, tm, seq):
    i = pl.program_id(0)
    gw = GROUP_W
    ya = ys5_ref[...] + s5d_ref[...] * s5u_ref[...]
    ga = _gelu_tanh(ya)
    ya = ga * _sigmoid(_dot(ga.astype(BF16), wglu_ref[...]))
    sc = sc_ref[...]
    first = (i * tm) % seq == 0
    halo = jnp.where(first, 0.0, sch_ref[...])
    cx = jnp.concatenate([halo[:, 2 * gw:] * halo[:, :gw], sc[:, 2 * gw:] * sc[:, :gw]], axis=0)
    conv = jnp.zeros((tm, gw), F32)
    for k in range(SCONV_K):
        off = HALO - (SCONV_K - 1) + k
        conv = conv + scw_ref[k:k + 1, :] * cx[off:off + tm, :]
    yc = sc[:, gw:2 * gw] * conv
    mg = mg_ref[...]
    acc = h_ref[...]
    for j, y in enumerate((ya, att_ref[...], yc, ssd_ref[...])):
        acc = acc + _dot(_group_norm(y, mg[:, j * gw:(j + 1) * gw]), wo_ref[j * gw:(j + 1) * gw, :])
    o_ref[...] = acc


def _merge(h, ys5, s5u, att, sc, ssd, s5_d, w_glu, sconv_w, merge_g, w_out, seq, tm=512):
    n = h.shape[0]
    per = tm // HALO

    def tile(wd):
        return pl.BlockSpec((tm, wd), lambda i: (i, 0))

    def const(shape):
        return pl.BlockSpec(shape, lambda i: (0, 0))

    return pl.pallas_call(
        functools.partial(_merge_kernel, tm=tm, seq=seq),
        grid=(n // tm,),
        in_specs=[tile(D_MODEL), tile(GROUP_W), tile(GROUP_W), tile(GROUP_W), tile(3 * GROUP_W),
                  pl.BlockSpec((HALO, 3 * GROUP_W), lambda i: (jnp.maximum(i * per - 1, 0), 0)),
                  tile(GROUP_W), const((1, GROUP_W)), const((GROUP_W, GROUP_W)),
                  const((SCONV_K, GROUP_W)), const((1, D_MODEL)), const((D_MODEL, D_MODEL))],
        out_specs=tile(D_MODEL),
        out_shape=jax.ShapeDtypeStruct((n, D_MODEL), F32),
        compiler_params=_cparams(("parallel",)),
        name="merge",
    )(h, ys5, s5u, att, sc, sc, ssd, s5_d[None, :], w_glu, sconv_w, merge_g[None, :], w_out)


def _ffn_kernel(h_ref, hh_ref, g_ref, wg_ref, wu_ref, cw_ref, wd_ref, o_ref, v_s, vh_s, acc_s, *, tm, seq):
    i = pl.program_id(0)
    j = pl.program_id(1)

    @pl.when(j == 0)
    def _():
        g = g_ref[...]
        v_s[...] = _rms(h_ref[...], g).astype(BF16)
        vh_s[...] = _rms(hh_ref[...], g).astype(BF16)
        acc_s[...] = jnp.zeros_like(acc_s)

    first = (i * tm) % seq == 0
    v = v_s[...]
    gate = _dot(v, wg_ref[...])
    gate_h = jnp.where(first, 0.0, _dot(vh_s[...], wg_ref[...]))
    ext = jnp.concatenate([gate_h, gate], axis=0)
    conv = jnp.zeros_like(gate)
    for k in range(FFN_CONV_K):
        off = HALO - (FFN_CONV_K - 1) + k
        conv = conv + cw_ref[k:k + 1, :] * ext[off:off + tm, :]
    act = (_silu(conv) * _dot(v, wu_ref[...])).astype(BF16)
    acc_s[...] += _dot(act, wd_ref[...])

    @pl.when(j == pl.num_programs(1) - 1)
    def _():
        o_ref[...] = h_ref[...] + acc_s[...]


def _ffn(h, g, w_gate, w_up, conv_w, w_down, seq, tm=512, fc=256):
    n = h.shape[0]
    per = tm // HALO
    return pl.pallas_call(
        functools.partial(_ffn_kernel, tm=tm, seq=seq),
        grid=(n // tm, D_FF // fc),
        in_specs=[pl.BlockSpec((tm, D_MODEL), lambda i, j: (i, 0)),
                  pl.BlockSpec((HALO, D_MODEL), lambda i, j: (jnp.maximum(i * per - 1, 0), 0)),
                  pl.BlockSpec((1, D_MODEL), lambda i, j: (0, 0)),
                  pl.BlockSpec((D_MODEL, fc), lambda i, j: (0, j)),
                  pl.BlockSpec((D_MODEL, fc), lambda i, j: (0, j)),
                  pl.BlockSpec((FFN_CONV_K, fc), lambda i, j: (0, j)),
                  pl.BlockSpec((fc, D_MODEL), lambda i, j: (j, 0))],
        out_specs=pl.BlockSpec((tm, D_MODEL), lambda i, j: (i, 0)),
        out_shape=jax.ShapeDtypeStruct((n, D_MODEL), F32),
        scratch_shapes=[pltpu.VMEM((tm, D_MODEL), BF16), pltpu.VMEM((HALO, D_MODEL), BF16),
                        pltpu.VMEM((tm, D_MODEL), F32)],
        compiler_params=_cparams(("parallel", "arbitrary")),
        name="ffn",
    )(h, h, g, w_gate, w_up, conv_w, w_down)


def _ple_kernel(h_ref, p_ref, g_ref, wg_ref, wp_ref, o_ref):
    h = h_ref[...]
    gate = _sigmoid(_dot(_rms(h, g_ref[...]).astype(BF16), wg_ref[...]))
    o_ref[...] = h + gate * _dot(p_ref[...].astype(BF16), wp_ref[...])


def _ple(h, p, g, w_gate, w_proj, tm=512):
    n = h.shape[0]
    return pl.pallas_call(
        _ple_kernel,
        grid=(n // tm,),
        in_specs=[pl.BlockSpec((tm, D_MODEL), lambda i: (i, 0)),
                  pl.BlockSpec((tm, PLE_DIM), lambda i: (i, 0)),
                  pl.BlockSpec((1, D_MODEL), lambda i: (0, 0)),
                  pl.BlockSpec((D_MODEL, D_MODEL), lambda i: (0, 0)),
                  pl.BlockSpec((PLE_DIM, D_MODEL), lambda i: (0, 0))],
        out_specs=pl.BlockSpec((tm, D_MODEL), lambda i: (i, 0)),
        out_shape=jax.ShapeDtypeStruct((n, D_MODEL), F32),
        compiler_params=_cparams(("parallel",)),
        name="ple",
    )(h, p, g, w_gate, w_proj)


def kernel(x, p, mix_norm_g, w_in, s5_a_re, s5_a_im, s5_log_dt, s5_b_re, s5_b_im, s5_c_re, s5_c_im, s5_d, s5_w_glu, moba_q_g, moba_k_g, sconv_w, ssd_conv_w, ssd_conv_b, ssd_dt_bias, ssd_a_log, ssd_d, merge_norm_g, w_out, ffn_norm_g, ffn_w_gate, ffn_w_up, ffn_conv_w, ffn_w_down, ple_norm_g, ple_w_gate, ple_w_proj):
    bsz, seq, _ = x.shape
    depth = w_in.shape[0]
    n = bsz * seq
    h = x.reshape(n, D_MODEL)
    for i in range(depth):
        w_in_p = jnp.pad(w_in[i], ((0, 0), (0, PROJ_PAD - w_in.shape[2]))).astype(BF16)
        s5u, qkv, sc, z, xbc, dt = _inproj(h, mix_norm_g[i][None, :], w_in_p)
        mats = _s5_param_mats(s5_a_re[i], s5_a_im[i], s5_log_dt[i], s5_b_re[i], s5_b_im[i],
                              s5_c_re[i], s5_c_im[i])
        ys5 = _s5_scan(s5u, mats, bsz, seq)
        att = _moba(qkv, moba_q_g[i][None, :], moba_k_g[i][None, :], bsz, seq)
        yd = _ssd(z, xbc, dt, ssd_conv_w[i], ssd_conv_b[i], ssd_dt_bias[i], ssd_a_log[i], ssd_d[i],
                  bsz, seq)
        h = _merge(h, ys5, s5u, att, sc, yd, s5_d[i], s5_w_glu[i].astype(BF16), sconv_w[i],
                   merge_norm_g[i], w_out[i].astype(BF16), seq)
        h = _ffn(h, ffn_norm_g[i][None, :], ffn_w_gate[i].astype(BF16), ffn_w_up[i].astype(BF16),
                 ffn_conv_w[i], ffn_w_down[i].astype(BF16), seq)
        h = _ple(h, p[i].reshape(n, PLE_DIM), ple_norm_g[i][None, :], ple_w_gate[i].astype(BF16),
                 ple_w_proj[i].astype(BF16))
    return h.reshape(bsz, seq, D_MODEL)
```

```python
import functools
import math

import jax
import jax.numpy as jnp
from jax import lax
from jax.experimental import pallas as pl
from jax.experimental.pallas import tpu as pltpu

F32 = jnp.float32
BF16 = jnp.bfloat16
EPS = 1e-6
NEG = -1e30
HIGHEST = lax.Precision.HIGHEST

D_MODEL = 1024
PLE_DIM = 256
GROUP_W = 256
N_MIXERS = 4
S5_CH = 16
S5_GROUPS = 16
S5_STATE = 64
S5_L = 8
S5_HG = 8
HEAD_DIM = 64
N_HEADS = 4
MOBA_BLOCK = 256
MOBA_TOPK = 3
SSD_GROUPS = 2
SSD_STATE = 128
SSD_CONV_K = 4
SSD_CHUNK = 128
SSD_CONV_CH = GROUP_W + 2 * SSD_GROUPS * SSD_STATE
SCONV_K = 3
D_FF = 2816
FFN_CONV_K = 3
HALO = 8
LANE = 128
PROJ_PAD = 2944
VMEM_LIMIT = 56 * 1024 * 1024


def _cparams(sem):
    return pltpu.CompilerParams(dimension_semantics=sem, vmem_limit_bytes=VMEM_LIMIT)


def _rms(x, g):
    return x * lax.rsqrt(jnp.mean(x * x, axis=-1, keepdims=True) + EPS) * g


def _sigmoid(x):
    return 1.0 / (1.0 + jnp.exp(-x))


def _silu(x):
    return x * _sigmoid(x)


def _dot(a, b):
    return jnp.dot(a, b, preferred_element_type=F32)


def _dot_nt(a, b, precision=None):
    return lax.dot_general(a, b, (((1,), (1,)), ((), ())), preferred_element_type=F32,
                           precision=precision)


def _inproj_kernel(h_ref, g_ref, w_ref, s5u_ref, qkv_ref, sc_ref, z_ref, xbc_ref, dt_ref, s5a_ref, s5b_ref):
    u = _rms(h_ref[...], g_ref[...]).astype(BF16)
    s5u = _dot(u, w_ref[:, 0:256])
    s5u_ref[...] = s5u
    s5a_ref[...] = s5u[:, :LANE].astype(BF16)
    s5b_ref[...] = s5u[:, LANE:].astype(BF16)
    qkv_ref[...] = _dot(u, w_ref[:, 256:1024])
    sc_ref[...] = _dot(u, w_ref[:, 1024:1792])
    z_ref[...] = _dot(u, w_ref[:, 1792:2048])
    xbc_ref[...] = _dot(u, w_ref[:, 2048:2816])
    dt_ref[...] = _dot(u, w_ref[:, 2816:PROJ_PAD])


def _inproj(h, g, w, tm=512):
    n = h.shape[0]
    widths = (256, 768, 768, 256, 768, LANE, LANE, LANE)
    dtypes = (F32,) * 6 + (BF16,) * 2
    return pl.pallas_call(
        _inproj_kernel,
        grid=(n // tm,),
        in_specs=[pl.BlockSpec((tm, D_MODEL), lambda i: (i, 0)),
                  pl.BlockSpec((1, D_MODEL), lambda i: (0, 0)),
                  pl.BlockSpec((D_MODEL, PROJ_PAD), lambda i: (0, 0))],
        out_specs=[pl.BlockSpec((tm, wd), lambda i: (i, 0)) for wd in widths],
        out_shape=[jax.ShapeDtypeStruct((n, wd), dt) for wd, dt in zip(widths, dtypes)],
        compiler_params=_cparams(("parallel",)),
        name="inproj",
    )(h, g, w)


def _s5_param_mats(a_re, a_im, log_dt, b_re, b_im, c_re, c_im):
    L = S5_L
    dt = jnp.exp(log_dt)[:, None]
    taus = jnp.arange(L + 1, dtype=F32)[None, :, None]
    mag = jnp.exp((a_re * dt)[:, None, :] * taus)
    ang = (a_im * dt)[:, None, :] * taus
    pw_r, pw_i = mag * jnp.cos(ang), mag * jnp.sin(ang)
    ab_r, ab_i = pw_r[:, 1], pw_i[:, 1]
    den = a_re * a_re + a_im * a_im
    nr = ab_r - 1.0
    coef_r = (nr * a_re + ab_i * a_im) / den
    coef_i = (ab_i * a_re - nr * a_im) / den
    bb_r = coef_r[..., None] * b_re - coef_i[..., None] * b_im
    bb_i = coef_r[..., None] * b_im + coef_i[..., None] * b_re
    rev_r, rev_i = pw_r[:, L - 1::-1], pw_i[:, L - 1::-1]
    m_r = jnp.einsum("gsp,gpc->gscp", rev_r, bb_r) - jnp.einsum("gsp,gpc->gscp", rev_i, bb_i)
    m_i = jnp.einsum("gsp,gpc->gscp", rev_r, bb_i) + jnp.einsum("gsp,gpc->gscp", rev_i, bb_r)
    m_r = m_r.reshape(S5_GROUPS, L * S5_CH, S5_STATE)
    m_i = m_i.reshape(S5_GROUPS, L * S5_CH, S5_STATE)
    cp_r = c_re[:, None] * pw_r[:, :, None, :] - c_im[:, None] * pw_i[:, :, None, :]
    cp_i = c_re[:, None] * pw_i[:, :, None, :] + c_im[:, None] * pw_r[:, :, None, :]
    kk = (jnp.einsum("gtcp,gpd->gtcd", cp_r[:, :L], bb_r, precision=HIGHEST)
          - jnp.einsum("gtcp,gpd->gtcd", cp_i[:, :L], bb_i, precision=HIGHEST))
    s_ix = jnp.arange(L)[:, None]
    l_ix = jnp.arange(L)[None, :]
    lag = l_ix - s_ix
    t_full = jnp.where((lag >= 0)[None, :, :, None, None], kk[:, jnp.clip(lag, 0, L - 1)], 0.0)
    t5 = t_full.transpose(0, 1, 4, 2, 3)
    p_r = cp_r[:, 1:].transpose(0, 3, 1, 2)
    p_i = (-cp_i[:, 1:]).transpose(0, 3, 1, 2)
    hg = S5_HG
    nh = S5_GROUPS // hg
    eye = jnp.eye(hg, dtype=F32)
    w = L * hg * S5_CH
    ns = hg * S5_STATE
    t_mat = jnp.einsum("hgsdlc,gk->hsgdlkc", t5.reshape(nh, hg, L, S5_CH, L, S5_CH), eye).reshape(nh, w, w)
    m_mat = jnp.concatenate(
        [jnp.einsum("hgsdp,gk->hsgdkp", m.reshape(nh, hg, L, S5_CH, S5_STATE), eye).reshape(nh, w, ns)
         for m in (m_r, m_i)], axis=-1)
    p_mat = jnp.concatenate(
        [jnp.einsum("hgplc,gk->hgplkc", q.reshape(nh, hg, S5_STATE, L, S5_CH), eye).reshape(nh, ns, w)
         for q in (p_r, p_i)], axis=1)
    al_r = pw_r[:, L].reshape(nh, 1, ns)
    al_i = pw_i[:, L].reshape(nh, 1, ns)
    return t_mat.astype(BF16), m_mat.astype(BF16), p_mat.astype(BF16), al_r, al_i


def _s5_kernel(u_ref, t_ref, m_ref, p_ref, alr_ref, ali_ref, y_ref, loc, sp, sr_s, si_s, *, steps, bsz):
    ns = S5_HG * S5_STATE

    @pl.when(pl.program_id(1) == 0)
    def _():
        sr_s[...] = jnp.zeros_like(sr_s)
        si_s[...] = jnp.zeros_like(si_s)

    u = u_ref[0]
    loc[...] = _dot(u, m_ref[0])
    ar = jnp.broadcast_to(alr_ref[0], (bsz, ns))
    ai = jnp.broadcast_to(ali_ref[0], (bsz, ns))

    def step(k, carry):
        sr, si = carry
        r0 = pl.multiple_of(k * bsz, bsz)
        sp[pl.ds(r0, bsz), 0:ns] = sr
        sp[pl.ds(r0, bsz), ns:2 * ns] = si
        lr = loc[pl.ds(r0, bsz), 0:ns]
        li = loc[pl.ds(r0, bsz), ns:2 * ns]
        return ar * sr - ai * si + lr, ar * si + ai * sr + li

    sr, si = lax.fori_loop(0, steps, step, (sr_s[...], si_s[...]))
    sr_s[...] = sr
    si_s[...] = si
    y_ref[0] = _dot(u, t_ref[0]) + _dot(sp[...].astype(BF16), p_ref[0])


def _s5_scan(s5_halves, mats, bsz, seq, steps=64):
    n_chunks = seq // S5_L
    rows = n_chunks * bsz
    w = S5_L * LANE
    ns = S5_HG * S5_STATE
    nh = len(s5_halves)
    u = jnp.stack([a.reshape(bsz, n_chunks, w).transpose(1, 0, 2).reshape(rows, w) for a in s5_halves])
    t_mat, m_mat, p_mat, al_r, al_i = mats
    rb = steps * bsz

    def hspec(a, b):
        return pl.BlockSpec((1, a, b), lambda hf, j: (hf, 0, 0))

    def rspec():
        return pl.BlockSpec((1, rb, w), lambda hf, j: (hf, j, 0))

    y = pl.pallas_call(
        functools.partial(_s5_kernel, steps=steps, bsz=bsz),
        grid=(nh, rows // rb),
        in_specs=[rspec(), hspec(w, w), hspec(w, 2 * ns), hspec(2 * ns, w), hspec(1, ns), hspec(1, ns)],
        out_specs=rspec(),
        out_shape=jax.ShapeDtypeStruct((nh, rows, w), F32),
        scratch_shapes=[pltpu.VMEM((rb, 2 * ns), F32), pltpu.VMEM((rb, 2 * ns), F32),
                        pltpu.VMEM((bsz, ns), F32), pltpu.VMEM((bsz, ns), F32)],
        compiler_params=_cparams(("parallel", "arbitrary")),
        name="s5_scan",
    )(u, t_mat, m_mat, p_mat, al_r, al_i)
    y = y.reshape(nh, n_chunks, bsz, S5_L, LANE).transpose(0, 2, 1, 3, 4)
    return [y[hf].reshape(bsz * seq, LANE) for hf in range(nh)]


def _moba_kernel(q_ref, k_ref, v_ref, qg_ref, kg_ref, o_ref, kx, vt, kmean, *, n_blocks):
    i = pl.program_id(1)
    blk = MOBA_BLOCK
    hd = HEAD_DIM
    seq = n_blocks * blk

    @pl.when(i == 0)
    def _():
        kg = kg_ref[...]
        lane = lax.broadcasted_iota(jnp.int32, (seq, hd), 1)
        kblk = lax.broadcasted_iota(jnp.int32, (seq, hd), 0) // blk
        onehot = jnp.where(lane == kblk, 1.0, 0.0).astype(BF16)
        for h in range(N_HEADS):
            kh = _rms(k_ref[:, h * hd:(h + 1) * hd], kg)
            kx[h, :, 0:hd] = kh.astype(BF16)
            kx[h, :, hd:2 * hd] = onehot
            kmean[h] = jnp.mean(kh.reshape(n_blocks, blk, hd), axis=1)
        for nb in range(n_blocks):
            vt[:, nb * blk:(nb + 1) * blk] = v_ref[nb * blk:(nb + 1) * blk, :].T.astype(BF16)

    q_t = q_ref[...].T
    qg = qg_ref[...]
    key_ix = lax.broadcasted_iota(jnp.int32, (blk, blk), 0)
    qry_ix = lax.broadcasted_iota(jnp.int32, (blk, blk), 1)
    causal = key_ix <= qry_ix
    bid = lax.broadcasted_iota(jnp.int32, (n_blocks, blk), 0)
    past = bid < i
    own0 = pl.multiple_of(i * blk, blk)

    qxs = []
    for h in range(N_HEADS):
        qh = q_t[h * hd:(h + 1) * hd, :]
        qn = qh * lax.rsqrt(jnp.mean(qh * qh, axis=0, keepdims=True) + EPS) * qg
        gate = jnp.dot(kmean[h], qn, preferred_element_type=F32, precision=HIGHEST)
        gate = jnp.where(past, gate, -jnp.inf)
        cnt = jnp.zeros((n_blocks, blk), F32)
        for m in range(n_blocks):
            gm = gate[m:m + 1, :]
            cnt = cnt + jnp.where(gm > gate, 1.0, jnp.where((gm == gate) & (bid > m), 1.0, 0.0))
        bias = jnp.where((bid == i) | (past & (cnt < MOBA_TOPK)), 0.0, NEG)
        qxs.append(jnp.concatenate([qn * (hd ** -0.5), bias, jnp.zeros((hd - n_blocks, blk), F32)],
                                   axis=0).astype(BF16))

    def scores(n):
        k0 = pl.multiple_of(n * blk, blk)
        return tuple(_dot(kx[h, pl.ds(k0, blk), :], qxs[h]) for h in range(N_HEADS))

    def values(n, ps):
        k0 = pl.multiple_of(n * blk, blk)
        return [_dot(vt[h * hd:(h + 1) * hd, pl.ds(k0, blk)], ps[h]) for h in range(N_HEADS)]

    s_own = scores(i)
    ps, init = [], []
    for h in range(N_HEADS):
        s = jnp.where(causal, s_own[h], NEG)
        m0 = jnp.max(s, axis=0, keepdims=True)
        p = jnp.exp(s - m0)
        init.append((m0, jnp.sum(p, axis=0, keepdims=True)))
        ps.append(p.astype(BF16))
    init = tuple((m0, l0, acc0) for (m0, l0), acc0 in zip(init, values(i, ps)))

    def body(n, carry):
        s_cur, stats = carry
        s_next = scores(jnp.minimum(n + 1, jnp.maximum(i - 1, 0)))
        ps, upd = [], []
        for h in range(N_HEADS):
            m_run, l_run, _ = stats[h]
            s = s_cur[h]
            m_new = jnp.maximum(m_run, jnp.max(s, axis=0, keepdims=True))
            alpha = jnp.exp(m_run - m_new)
            p = jnp.exp(s - m_new)
            upd.append((m_new, alpha * l_run + jnp.sum(p, axis=0, keepdims=True), alpha))
            ps.append(p.astype(BF16))
        pv = values(n, ps)
        return s_next, tuple((m_new, l_new, alpha * stats[h][2] + pv[h])
                             for h, (m_new, l_new, alpha) in enumerate(upd))

    _, fin = lax.fori_loop(0, i, body, (scores(0), init))
    out_t = jnp.concatenate([acc / l_fin for (_, l_fin, acc) in fin], axis=0)
    o_ref[...] = out_t.T


def _moba(qkv, q_g, k_g, bsz, seq):
    n_blocks = seq // MOBA_BLOCK
    n = bsz * seq
    return pl.pallas_call(
        functools.partial(_moba_kernel, n_blocks=n_blocks),
        grid=(bsz, n_blocks),
        in_specs=[pl.BlockSpec((MOBA_BLOCK, GROUP_W), lambda b, i: (b * n_blocks + i, 0)),
                  pl.BlockSpec((seq, GROUP_W), lambda b, i: (b, 1)),
                  pl.BlockSpec((seq, GROUP_W), lambda b, i: (b, 2)),
                  pl.BlockSpec((HEAD_DIM, 1), lambda b, i: (0, 0)),
                  pl.BlockSpec((1, HEAD_DIM), lambda b, i: (0, 0))],
        out_specs=pl.BlockSpec((MOBA_BLOCK, GROUP_W), lambda b, i: (b * n_blocks + i, 0)),
        out_shape=jax.ShapeDtypeStruct((n, GROUP_W), F32),
        scratch_shapes=[pltpu.VMEM((N_HEADS, seq, 2 * HEAD_DIM), BF16),
                        pltpu.VMEM((GROUP_W, seq), BF16),
                        pltpu.VMEM((N_HEADS, n_blocks, HEAD_DIM), F32)],
        compiler_params=_cparams(("parallel", "arbitrary")),
        name="moba",
    )(qkv, qkv, qkv, q_g.reshape(HEAD_DIM, 1), k_g.reshape(1, HEAD_DIM))


def _ssd_kernel(z_ref, xbc_ref, halo_ref, dt_ref, cw_ref, cb_ref, dtb_ref, a_ref, d_ref, o_ref, state):
    c = pl.program_id(1)
    lc = SSD_CHUNK
    gn = SSD_STATE

    @pl.when(c == 0)
    def _():
        state[...] = jnp.zeros_like(state)

    halo = jnp.where(c == 0, 0.0, halo_ref[...])
    ext = jnp.concatenate([halo, xbc_ref[...]], axis=0)
    conv = cb_ref[...]
    for k in range(SSD_CONV_K):
        off = HALO - (SSD_CONV_K - 1) + k
        conv = conv + cw_ref[k:k + 1, :] * ext[off:off + lc, :]
    xbc = _silu(conv)
    xs = xbc[:, :GROUP_W]
    dtr = dt_ref[...] + dtb_ref[...]
    dt = jnp.maximum(dtr, 0.0) + jnp.log(1.0 + jnp.exp(-jnp.abs(dtr)))
    adt = dt * a_ref[...]
    row = lax.broadcasted_iota(jnp.int32, (lc, lc), 0)
    col = lax.broadcasted_iota(jnp.int32, (lc, lc), 1)
    causal = col <= row
    tril = jnp.where(causal, 1.0, 0.0)
    cs = jnp.dot(tril, adt, preferred_element_type=F32, precision=HIGHEST)
    cs_t = cs.T
    z = z_ref[...]
    dd = d_ref[...]
    for h in range(N_HEADS):
        g = h // (N_HEADS // SSD_GROUPS)
        bm = xbc[:, GROUP_W + g * gn:GROUP_W + (g + 1) * gn]
        cm = xbc[:, GROUP_W + SSD_GROUPS * gn + g * gn:GROUP_W + SSD_GROUPS * gn + (g + 1) * gn]
        cmb = cm.astype(BF16)
        cs_col = cs[:, h:h + 1]
        cs_row = cs_t[h:h + 1, :]
        seg = cs_col - cs_row
        decay = jnp.where(causal, jnp.exp(jnp.where(causal, seg, 0.0)), 0.0)
        scores = (_dot_nt(cmb, bm.astype(BF16)) * decay).astype(BF16)
        hs = slice(h * HEAD_DIM, (h + 1) * HEAD_DIM)
        x_h = xs[:, hs]
        xd = (x_h * dt[:, h:h + 1]).astype(BF16)
        y = _dot(scores, xd)
        s_prev = state[h]
        y = y + _dot(cmb, s_prev.astype(BF16)) * jnp.exp(cs_col)
        cs_last = cs[lc - 1:lc, h:h + 1]
        bd = bm * jnp.exp(cs_last - cs_col)
        state[h] = jnp.exp(cs_last) * s_prev + _dot(bd.T.astype(BF16), xd)
        y = y + dd[:, hs] * x_h
        o_ref[:, hs] = y * _silu(z[:, hs])


def _ssd(z, xbc, dt, conv_w, conv_b, dt_bias, a_log, d_skip, bsz, seq):
    n = bsz * seq
    nc = seq // SSD_CHUNK
    per = SSD_CHUNK // HALO
    dtb = jnp.zeros((1, LANE), F32).at[0, :N_HEADS].set(dt_bias)
    a = jnp.zeros((1, LANE), F32).at[0, :N_HEADS].set(-jnp.exp(a_log))
    dfull = jnp.repeat(d_skip, HEAD_DIM)[None, :]

    def const(shape):
        return pl.BlockSpec(shape, lambda b, c: (0, 0))

    return pl.pallas_call(
        _ssd_kernel,
        grid=(bsz, nc),
        in_specs=[pl.BlockSpec((SSD_CHUNK, GROUP_W), lambda b, c: (b * nc + c, 0)),
                  pl.BlockSpec((SSD_CHUNK, SSD_CONV_CH), lambda b, c: (b * nc + c, 0)),
                  pl.BlockSpec((HALO, SSD_CONV_CH), lambda b, c: (jnp.maximum((b * nc + c) * per - 1, 0), 0)),
                  pl.BlockSpec((SSD_CHUNK, LANE), lambda b, c: (b * nc + c, 0)),
                  const((SSD_CONV_K, SSD_CONV_CH)), const((1, SSD_CONV_CH)), const((1, LANE)),
                  const((1, LANE)), const((1, GROUP_W))],
        out_specs=pl.BlockSpec((SSD_CHUNK, GROUP_W), lambda b, c: (b * nc + c, 0)),
        out_shape=jax.ShapeDtypeStruct((n, GROUP_W), F32),
        scratch_shapes=[pltpu.VMEM((N_HEADS, SSD_STATE, HEAD_DIM), F32)],
        compiler_params=_cparams(("parallel", "arbitrary")),
        name="ssd",
    )(z, xbc, xbc, dt, conv_w, conv_b[None, :], dtb, a, dfull)


def _gelu_tanh(x):
    return 0.5 * x * (1.0 + jnp.tanh(math.sqrt(2.0 / math.pi) * (x + 0.044715 * (x * x * x))))


def _group_norm(y, g):
    return (y * lax.rsqrt(jnp.mean(y * y, axis=-1, keepdims=True) + EPS) * g).astype(BF16)


def _merge_kernel(h_ref, ys5a_ref, ys5b_ref, s5u_ref, att_ref, sc_ref, sch_ref, ssd_ref, s5d_ref, wglu_ref,
                  scw_ref, mg_ref, wo_ref, o_ref, *, tm, seq):
    i = pl.program_id(0)
    gw = GROUP_W
    ya = jnp.concatenate([ys5a_ref[...], ys5b_ref[...]], axis=-1) + s5d_ref[...] * s5u_ref[...]
    ga = _gelu_tanh(ya)
    ya = ga * _sigmoid(_dot(ga.astype(BF16), wglu_ref[...]))
    sc = sc_ref[...]
    first = (i * tm) % seq == 0
    halo = jnp.where(first, 0.0, sch_ref[...])
    cx = jnp.concatenate([halo[:, 2 * gw:] * halo[:, :gw], sc[:, 2 * gw:] * sc[:, :gw]], axis=0)
    conv = jnp.zeros((tm, gw), F32)
    for k in range(SCONV_K):
        off = HALO - (SCONV_K - 1) + k
        conv = conv + scw_ref[k:k + 1, :] * cx[off:off + tm, :]
    yc = sc[:, gw:2 * gw] * conv
    mg = mg_ref[...]
    acc = h_ref[...]
    for j, y in enumerate((ya, att_ref[...], yc, ssd_ref[...])):
        acc = acc + _dot(_group_norm(y, mg[:, j * gw:(j + 1) * gw]), wo_ref[j * gw:(j + 1) * gw, :])
    o_ref[...] = acc


def _merge(h, ys5, s5u, att, sc, ssd, s5_d, w_glu, sconv_w, merge_g, w_out, seq, tm=512):
    n = h.shape[0]
    per = tm // HALO

    def tile(wd):
        return pl.BlockSpec((tm, wd), lambda i: (i, 0))

    def const(shape):
        return pl.BlockSpec(shape, lambda i: (0, 0))

    return pl.pallas_call(
        functools.partial(_merge_kernel, tm=tm, seq=seq),
        grid=(n // tm,),
        in_specs=[tile(D_MODEL), tile(LANE), tile(LANE), tile(GROUP_W), tile(GROUP_W), tile(3 * GROUP_W),
                  pl.BlockSpec((HALO, 3 * GROUP_W), lambda i: (jnp.maximum(i * per - 1, 0), 0)),
                  tile(GROUP_W), const((1, GROUP_W)), const((GROUP_W, GROUP_W)),
                  const((SCONV_K, GROUP_W)), const((1, D_MODEL)), const((D_MODEL, D_MODEL))],
        out_specs=tile(D_MODEL),
        out_shape=jax.ShapeDtypeStruct((n, D_MODEL), F32),
        compiler_params=_cparams(("parallel",)),
        name="merge",
    )(h, ys5[0], ys5[1], s5u, att, sc, sc, ssd, s5_d[None, :], w_glu, sconv_w, merge_g[None, :], w_out)


def _ffn_kernel(h_ref, hh_ref, g_ref, wg_ref, wu_ref, cw_ref, wd_ref, o_ref, v_s, vh_s, acc_s, *, tm, seq):
    i = pl.program_id(0)
    j = pl.program_id(1)

    @pl.when(j == 0)
    def _():
        g = g_ref[...]
        v_s[...] = _rms(h_ref[...], g).astype(BF16)
        vh_s[...] = _rms(hh_ref[...], g).astype(BF16)
        acc_s[...] = jnp.zeros_like(acc_s)

    first = (i * tm) % seq == 0
    v = v_s[...]
    gate = _dot(v, wg_ref[...])
    gate_h = jnp.where(first, 0.0, _dot(vh_s[...], wg_ref[...]))
    ext = jnp.concatenate([gate_h, gate], axis=0)
    conv = jnp.zeros_like(gate)
    for k in range(FFN_CONV_K):
        off = HALO - (FFN_CONV_K - 1) + k
        conv = conv + cw_ref[k:k + 1, :] * ext[off:off + tm, :]
    act = (_silu(conv) * _dot(v, wu_ref[...])).astype(BF16)
    acc_s[...] += _dot(act, wd_ref[...])

    @pl.when(j == pl.num_programs(1) - 1)
    def _():
        o_ref[...] = h_ref[...] + acc_s[...]


def _ffn(h, g, w_gate, w_up, conv_w, w_down, seq, tm=512, fc=256):
    n = h.shape[0]
    per = tm // HALO
    return pl.pallas_call(
        functools.partial(_ffn_kernel, tm=tm, seq=seq),
        grid=(n // tm, D_FF // fc),
        in_specs=[pl.BlockSpec((tm, D_MODEL), lambda i, j: (i, 0)),
                  pl.BlockSpec((HALO, D_MODEL), lambda i, j: (jnp.maximum(i * per - 1, 0), 0)),
                  pl.BlockSpec((1, D_MODEL), lambda i, j: (0, 0)),
                  pl.BlockSpec((D_MODEL, fc), lambda i, j: (0, j)),
                  pl.BlockSpec((D_MODEL, fc), lambda i, j: (0, j)),
                  pl.BlockSpec((FFN_CONV_K, fc), lambda i, j: (0, j)),
                  pl.BlockSpec((fc, D_MODEL), lambda i, j: (j, 0))],
        out_specs=pl.BlockSpec((tm, D_MODEL), lambda i, j: (i, 0)),
        out_shape=jax.ShapeDtypeStruct((n, D_MODEL), F32),
        scratch_shapes=[pltpu.VMEM((tm, D_MODEL), BF16), pltpu.VMEM((HALO, D_MODEL), BF16),
                        pltpu.VMEM((tm, D_MODEL), F32)],
        compiler_params=_cparams(("parallel", "arbitrary")),
        name="ffn",
    )(h, h, g, w_gate, w_up, conv_w, w_down)


def _ple_kernel(h_ref, p_ref, g_ref, wg_ref, wp_ref, o_ref):
    h = h_ref[...]
    gate = _sigmoid(_dot(_rms(h, g_ref[...]).astype(BF16), wg_ref[...]))
    o_ref[...] = h + gate * _dot(p_ref[...].astype(BF16), wp_ref[...])


def _ple(h, p, g, w_gate, w_proj, tm=512):
    n = h.shape[0]
    return pl.pallas_call(
        _ple_kernel,
        grid=(n // tm,),
        in_specs=[pl.BlockSpec((tm, D_MODEL), lambda i: (i, 0)),
                  pl.BlockSpec((tm, PLE_DIM), lambda i: (i, 0)),
                  pl.BlockSpec((1, D_MODEL), lambda i: (0, 0)),
                  pl.BlockSpec((D_MODEL, D_MODEL), lambda i: (0, 0)),
                  pl.BlockSpec((PLE_DIM, D_MODEL), lambda i: (0, 0))],
        out_specs=pl.BlockSpec((tm, D_MODEL), lambda i: (i, 0)),
        out_shape=jax.ShapeDtypeStruct((n, D_MODEL), F32),
        compiler_params=_cparams(("parallel",)),
        name="ple",
    )(h, p, g, w_gate, w_proj)


def kernel(x, p, mix_norm_g, w_in, s5_a_re, s5_a_im, s5_log_dt, s5_b_re, s5_b_im, s5_c_re, s5_c_im, s5_d, s5_w_glu, moba_q_g, moba_k_g, sconv_w, ssd_conv_w, ssd_conv_b, ssd_dt_bias, ssd_a_log, ssd_d, merge_norm_g, w_out, ffn_norm_g, ffn_w_gate, ffn_w_up, ffn_conv_w, ffn_w_down, ple_norm_g, ple_w_gate, ple_w_proj):
    bsz, seq, _ = x.shape
    depth = w_in.shape[0]
    n = bsz * seq
    h = x.reshape(n, D_MODEL)
    for i in range(depth):
        w_in_p = jnp.pad(w_in[i], ((0, 0), (0, PROJ_PAD - w_in.shape[2]))).astype(BF16)
        s5u, qkv, sc, z, xbc, dt, s5a, s5b = _inproj(h, mix_norm_g[i][None, :], w_in_p)
        mats = _s5_param_mats(s5_a_re[i], s5_a_im[i], s5_log_dt[i], s5_b_re[i], s5_b_im[i],
                              s5_c_re[i], s5_c_im[i])
        ys5 = _s5_scan((s5a, s5b), mats, bsz, seq)
        att = _moba(qkv, moba_q_g[i][None, :], moba_k_g[i][None, :], bsz, seq)
        yd = _ssd(z, xbc, dt, ssd_conv_w[i], ssd_conv_b[i], ssd_dt_bias[i], ssd_a_log[i], ssd_d[i],
                  bsz, seq)
        h = _merge(h, ys5, s5u, att, sc, yd, s5_d[i], s5_w_glu[i].astype(BF16), sconv_w[i],
                   merge_norm_g[i], w_out[i].astype(BF16), seq)
        h = _ffn(h, ffn_norm_g[i][None, :], ffn_w_gate[i].astype(BF16), ffn_w_up[i].astype(BF16),
                 ffn_conv_w[i], ffn_w_down[i].astype(BF16), seq)
        h = _ple(h, p[i].reshape(n, PLE_DIM), ple_norm_g[i][None, :], ple_w_gate[i].astype(BF16),
                 ple_w_proj[i].astype(BF16))
    return h.reshape(bsz, seq, D_MODEL)
```

```python
import functools
import math

import jax
import jax.numpy as jnp
from jax import lax
from jax.experimental import pallas as pl
from jax.experimental.pallas import tpu as pltpu

F32 = jnp.float32
BF16 = jnp.bfloat16
EPS = 1e-6
NEG = -1e30
HIGHEST = lax.Precision.HIGHEST

D_MODEL = 1024
PLE_DIM = 256
GROUP_W = 256
N_MIXERS = 4
S5_CH = 16
S5_GROUPS = 16
S5_STATE = 64
S5_L = 8
S5_HG = 8
HEAD_DIM = 64
N_HEADS = 4
MOBA_BLOCK = 256
MOBA_TOPK = 3
SSD_GROUPS = 2
SSD_STATE = 128
SSD_CONV_K = 4
SSD_CHUNK = 128
SSD_STEP_CHUNKS = 2
SSD_CONV_CH = GROUP_W + 2 * SSD_GROUPS * SSD_STATE
SCONV_K = 3
D_FF = 2816
FFN_CONV_K = 3
FFN_CHUNKS = 2
HALO = 8
LANE = 128
PROJ_PAD = 2944
VMEM_LIMIT = 56 * 1024 * 1024


def _cparams(sem):
    return pltpu.CompilerParams(dimension_semantics=sem, vmem_limit_bytes=VMEM_LIMIT)


def _rms(x, g):
    return x * lax.rsqrt(jnp.mean(x * x, axis=-1, keepdims=True) + EPS) * g


def _sigmoid(x):
    return 0.5 * jnp.tanh(0.5 * x) + 0.5


def _silu(x):
    return x * _sigmoid(x)


def _dot(a, b):
    return jnp.dot(a, b, preferred_element_type=F32)


def _dot_nt(a, b, precision=None):
    return lax.dot_general(a, b, (((1,), (1,)), ((), ())), preferred_element_type=F32,
                           precision=precision)


def _inproj_kernel(h_ref, g_ref, w_ref, s5u_ref, qkv_ref, sc_ref, z_ref, xbc_ref, dt_ref, s5a_ref, s5b_ref):
    u = _rms(h_ref[...], g_ref[...]).astype(BF16)
    s5u = _dot(u, w_ref[:, 0:256])
    s5u_ref[...] = s5u
    s5a_ref[...] = s5u[:, :LANE].astype(BF16)
    s5b_ref[...] = s5u[:, LANE:].astype(BF16)
    qkv_ref[...] = _dot(u, w_ref[:, 256:1024])
    sc_ref[...] = _dot(u, w_ref[:, 1024:1792])
    z_ref[...] = _dot(u, w_ref[:, 1792:2048])
    xbc_ref[...] = _dot(u, w_ref[:, 2048:2816])
    dt_ref[...] = _dot(u, w_ref[:, 2816:PROJ_PAD])


def _inproj(h, g, w, tm=512):
    n = h.shape[0]
    widths = (256, 768, 768, 256, 768, LANE, LANE, LANE)
    dtypes = (F32,) * 6 + (BF16,) * 2
    return pl.pallas_call(
        _inproj_kernel,
        grid=(n // tm,),
        in_specs=[pl.BlockSpec((tm, D_MODEL), lambda i: (i, 0)),
                  pl.BlockSpec((1, D_MODEL), lambda i: (0, 0)),
                  pl.BlockSpec((D_MODEL, PROJ_PAD), lambda i: (0, 0))],
        out_specs=[pl.BlockSpec((tm, wd), lambda i: (i, 0)) for wd in widths],
        out_shape=[jax.ShapeDtypeStruct((n, wd), dt) for wd, dt in zip(widths, dtypes)],
        compiler_params=_cparams(("parallel",)),
        name="inproj",
    )(h, g, w)


def _s5_param_mats(a_re, a_im, log_dt, b_re, b_im, c_re, c_im):
    L = S5_L
    dt = jnp.exp(log_dt)[:, None]
    taus = jnp.arange(L + 1, dtype=F32)[None, :, None]
    mag = jnp.exp((a_re * dt)[:, None, :] * taus)
    ang = (a_im * dt)[:, None, :] * taus
    pw_r, pw_i = mag * jnp.cos(ang), mag * jnp.sin(ang)
    ab_r, ab_i = pw_r[:, 1], pw_i[:, 1]
    den = a_re * a_re + a_im * a_im
    nr = ab_r - 1.0
    coef_r = (nr * a_re + ab_i * a_im) / den
    coef_i = (ab_i * a_re - nr * a_im) / den
    bb_r = coef_r[..., None] * b_re - coef_i[..., None] * b_im
    bb_i = coef_r[..., None] * b_im + coef_i[..., None] * b_re
    rev_r, rev_i = pw_r[:, L - 1::-1], pw_i[:, L - 1::-1]
    m_r = jnp.einsum("gsp,gpc->gscp", rev_r, bb_r) - jnp.einsum("gsp,gpc->gscp", rev_i, bb_i)
    m_i = jnp.einsum("gsp,gpc->gscp", rev_r, bb_i) + jnp.einsum("gsp,gpc->gscp", rev_i, bb_r)
    m_r = m_r.reshape(S5_GROUPS, L * S5_CH, S5_STATE)
    m_i = m_i.reshape(S5_GROUPS, L * S5_CH, S5_STATE)
    cp_r = c_re[:, None] * pw_r[:, :, None, :] - c_im[:, None] * pw_i[:, :, None, :]
    cp_i = c_re[:, None] * pw_i[:, :, None, :] + c_im[:, None] * pw_r[:, :, None, :]
    kk = (jnp.einsum("gtcp,gpd->gtcd", cp_r[:, :L], bb_r, precision=HIGHEST)
          - jnp.einsum("gtcp,gpd->gtcd", cp_i[:, :L], bb_i, precision=HIGHEST))
    s_ix = jnp.arange(L)[:, None]
    l_ix = jnp.arange(L)[None, :]
    lag = l_ix - s_ix
    t_full = jnp.where((lag >= 0)[None, :, :, None, None], kk[:, jnp.clip(lag, 0, L - 1)], 0.0)
    t5 = t_full.transpose(0, 1, 4, 2, 3)
    p_r = cp_r[:, 1:].transpose(0, 3, 1, 2)
    p_i = (-cp_i[:, 1:]).transpose(0, 3, 1, 2)
    hg = S5_HG
    nh = S5_GROUPS // hg
    eye = jnp.eye(hg, dtype=F32)
    w = L * hg * S5_CH
    ns = hg * S5_STATE
    t_mat = jnp.einsum("hgsdlc,gk->hsgdlkc", t5.reshape(nh, hg, L, S5_CH, L, S5_CH), eye).reshape(nh, w, w)
    m_mat = jnp.concatenate(
        [jnp.einsum("hgsdp,gk->hsgdkp", m.reshape(nh, hg, L, S5_CH, S5_STATE), eye).reshape(nh, w, ns)
         for m in (m_r, m_i)], axis=-1)
    p_mat = jnp.concatenate(
        [jnp.einsum("hgplc,gk->hgplkc", q.reshape(nh, hg, S5_STATE, L, S5_CH), eye).reshape(nh, ns, w)
         for q in (p_r, p_i)], axis=1)
    al_r = pw_r[:, L].reshape(nh, 1, ns)
    al_i = pw_i[:, L].reshape(nh, 1, ns)
    return t_mat.astype(BF16), m_mat.astype(BF16), p_mat.astype(BF16), al_r, al_i


def _s5_kernel(u_ref, t_ref, m_ref, p_ref, alr_ref, ali_ref, y_ref, loc, sp, sr_s, si_s, *, steps, bsz):
    ns = S5_HG * S5_STATE
    w = S5_L * LANE

    @pl.when(pl.program_id(0) == 0)
    def _():
        sr_s[...] = jnp.zeros_like(sr_s)
        si_s[...] = jnp.zeros_like(si_s)

    nt = ns // LANE
    u = u_ref[...].reshape(bsz * steps, w)
    loc_v = _dot(u, m_ref[0])
    for j in range(2 * nt):
        loc[j] = loc_v[:, j * LANE:(j + 1) * LANE]
    ar = [jnp.broadcast_to(alr_ref[0, :, j * LANE:(j + 1) * LANE], (bsz, LANE)) for j in range(nt)]
    ai = [jnp.broadcast_to(ali_ref[0, :, j * LANE:(j + 1) * LANE], (bsz, LANE)) for j in range(nt)]

    def step(k, carry):
        sr, si = carry
        rows = pl.ds(k, bsz, stride=steps)
        nr, ni = [], []
        for j in range(nt):
            sp[j, rows, :] = sr[j]
            sp[nt + j, rows, :] = si[j]
            nr.append(ar[j] * sr[j] - ai[j] * si[j] + loc[j, rows, :])
            ni.append(ar[j] * si[j] + ai[j] * sr[j] + loc[nt + j, rows, :])
        return tuple(nr), tuple(ni)

    init = (tuple(sr_s[:, j * LANE:(j + 1) * LANE] for j in range(nt)),
            tuple(si_s[:, j * LANE:(j + 1) * LANE] for j in range(nt)))
    sr, si = lax.fori_loop(0, steps, step, init)
    for j in range(nt):
        sr_s[:, j * LANE:(j + 1) * LANE] = sr[j]
        si_s[:, j * LANE:(j + 1) * LANE] = si[j]
    s_prev = jnp.concatenate([sp[j] for j in range(2 * nt)], axis=1).astype(BF16)
    y = _dot(u, t_ref[0]) + _dot(s_prev, p_ref[0])
    y_ref[...] = y.reshape(bsz, steps, w)


def _s5_scan(s5_halves, mats, bsz, seq, steps=64):
    n_chunks = seq // S5_L
    w = S5_L * LANE
    ns = S5_HG * S5_STATE
    t_mat, m_mat, p_mat, al_r, al_i = mats
    outs = []
    for hf, half in enumerate(s5_halves):
        def hspec(a, b, hf=hf):
            return pl.BlockSpec((1, a, b), lambda j: (hf, 0, 0))

        rspec = pl.BlockSpec((bsz, steps, w), lambda j: (0, j, 0))
        y = pl.pallas_call(
            functools.partial(_s5_kernel, steps=steps, bsz=bsz),
            grid=(n_chunks // steps,),
            in_specs=[rspec, hspec(w, w), hspec(w, 2 * ns), hspec(2 * ns, w), hspec(1, ns), hspec(1, ns)],
            out_specs=rspec,
            out_shape=jax.ShapeDtypeStruct((bsz, n_chunks, w), F32),
            scratch_shapes=[pltpu.VMEM((2 * ns // LANE, bsz * steps, LANE), F32),
                            pltpu.VMEM((2 * ns // LANE, bsz * steps, LANE), F32),
                            pltpu.VMEM((bsz, ns), F32), pltpu.VMEM((bsz, ns), F32)],
            compiler_params=_cparams(("arbitrary",)),
            name="s5_scan",
        )(half.reshape(bsz, n_chunks, w), t_mat, m_mat, p_mat, al_r, al_i)
        outs.append(y.reshape(bsz * seq, LANE))
    return outs


def _moba_kernel(q_ref, k_ref, v_ref, qg_ref, kg_ref, o_ref, kx, vt, kmean, *, n_blocks):
    i = pl.program_id(1)
    blk = MOBA_BLOCK
    hd = HEAD_DIM
    seq = n_blocks * blk

    @pl.when(i == 0)
    def _():
        kg = kg_ref[...]
        lane = lax.broadcasted_iota(jnp.int32, (seq, hd), 1)
        kblk = lax.broadcasted_iota(jnp.int32, (seq, hd), 0) // blk
        onehot = jnp.where(lane == kblk, 1.0, 0.0).astype(BF16)
        for h in range(N_HEADS):
            kh = _rms(k_ref[:, h * hd:(h + 1) * hd], kg)
            kx[h, :, 0:hd] = kh.astype(BF16)
            kx[h, :, hd:2 * hd] = onehot
            kmean[h] = jnp.mean(kh.reshape(n_blocks, blk, hd), axis=1)
        for nb in range(n_blocks):
            vt[:, nb * blk:(nb + 1) * blk] = v_ref[nb * blk:(nb + 1) * blk, :].T.astype(BF16)

    q_t = q_ref[...].T
    qg = qg_ref[...]
    key_ix = lax.broadcasted_iota(jnp.int32, (blk, blk), 0)
    qry_ix = lax.broadcasted_iota(jnp.int32, (blk, blk), 1)
    causal = key_ix <= qry_ix
    bid = lax.broadcasted_iota(jnp.int32, (n_blocks, blk), 0)
    past = bid < i
    own0 = pl.multiple_of(i * blk, blk)

    qxs = []
    for h in range(N_HEADS):
        qh = q_t[h * hd:(h + 1) * hd, :]
        qn = qh * lax.rsqrt(jnp.mean(qh * qh, axis=0, keepdims=True) + EPS) * qg
        gate = jnp.dot(kmean[h], qn, preferred_element_type=F32, precision=HIGHEST)
        gate = jnp.where(past, gate, -jnp.inf)
        cnt = jnp.zeros((n_blocks, blk), F32)
        for m in range(n_blocks):
            gm = gate[m:m + 1, :]
            cnt = cnt + jnp.where(gm > gate, 1.0, jnp.where((gm == gate) & (bid > m), 1.0, 0.0))
        bias = jnp.where((bid == i) | (past & (cnt < MOBA_TOPK)), 0.0, NEG)
        qxs.append(jnp.concatenate([qn * (hd ** -0.5), bias, jnp.zeros((hd - n_blocks, blk), F32)],
                                   axis=0).astype(BF16))

    def scores(n):
        k0 = pl.multiple_of(n * blk, blk)
        return tuple(_dot(kx[h, pl.ds(k0, blk), :], qxs[h]) for h in range(N_HEADS))

    def values(n, ps):
        k0 = pl.multiple_of(n * blk, blk)
        return [_dot(vt[h * hd:(h + 1) * hd, pl.ds(k0, blk)], ps[h]) for h in range(N_HEADS)]

    s_own = scores(i)
    ps, init = [], []
    for h in range(N_HEADS):
        s = jnp.where(causal, s_own[h], NEG)
        m0 = jnp.max(s, axis=0, keepdims=True)
        p = jnp.exp(s - m0)
        init.append((m0, jnp.sum(p, axis=0, keepdims=True)))
        ps.append(p.astype(BF16))
    init = tuple((m0, l0, acc0) for (m0, l0), acc0 in zip(init, values(i, ps)))

    def body(n, carry):
        s_cur, stats = carry
        s_next = scores(jnp.minimum(n + 1, jnp.maximum(i - 1, 0)))
        ps, upd = [], []
        for h in range(N_HEADS):
            m_run, l_run, _ = stats[h]
            s = s_cur[h]
            m_new = jnp.maximum(m_run, jnp.max(s, axis=0, keepdims=True))
            alpha = jnp.exp(m_run - m_new)
            p = jnp.exp(s - m_new)
            upd.append((m_new, alpha * l_run + jnp.sum(p, axis=0, keepdims=True), alpha))
            ps.append(p.astype(BF16))
        pv = values(n, ps)
        return s_next, tuple((m_new, l_new, alpha * stats[h][2] + pv[h])
                             for h, (m_new, l_new, alpha) in enumerate(upd))

    _, fin = lax.fori_loop(0, i, body, (scores(0), init))
    out_t = jnp.concatenate([acc / l_fin for (_, l_fin, acc) in fin], axis=0)
    o_ref[...] = out_t.T


def _moba(qkv, q_g, k_g, bsz, seq):
    n_blocks = seq // MOBA_BLOCK
    n = bsz * seq
    return pl.pallas_call(
        functools.partial(_moba_kernel, n_blocks=n_blocks),
        grid=(bsz, n_blocks),
        in_specs=[pl.BlockSpec((MOBA_BLOCK, GROUP_W), lambda b, i: (b * n_blocks + i, 0)),
                  pl.BlockSpec((seq, GROUP_W), lambda b, i: (b, 1)),
                  pl.BlockSpec((seq, GROUP_W), lambda b, i: (b, 2)),
                  pl.BlockSpec((HEAD_DIM, 1), lambda b, i: (0, 0)),
                  pl.BlockSpec((1, HEAD_DIM), lambda b, i: (0, 0))],
        out_specs=pl.BlockSpec((MOBA_BLOCK, GROUP_W), lambda b, i: (b * n_blocks + i, 0)),
        out_shape=jax.ShapeDtypeStruct((n, GROUP_W), F32),
        scratch_shapes=[pltpu.VMEM((N_HEADS, seq, 2 * HEAD_DIM), BF16),
                        pltpu.VMEM((GROUP_W, seq), BF16),
                        pltpu.VMEM((N_HEADS, n_blocks, HEAD_DIM), F32)],
        compiler_params=_cparams(("parallel", "arbitrary")),
        name="moba",
    )(qkv, qkv, qkv, q_g.reshape(HEAD_DIM, 1), k_g.reshape(1, HEAD_DIM))


def _shift_rows(x, halo, k):
    if k == 0:
        return x
    rolled = pltpu.roll(x, k, axis=0)
    row = lax.broadcasted_iota(jnp.int32, (HALO, x.shape[1]), 0)
    head = jnp.where(row < k, pltpu.roll(halo, k, axis=0), rolled[:HALO])
    return jnp.concatenate([head, rolled[HALO:]], axis=0)


def _ssd_kernel(z_ref, xbc_ref, halo_ref, dt_ref, cw_ref, cb_ref, dtb_ref, a_ref, d_ref, o_ref, state):
    c = pl.program_id(1)
    lc = SSD_CHUNK
    gn = SSD_STATE

    @pl.when(c == 0)
    def _():
        state[...] = jnp.zeros_like(state)

    halo = jnp.where(c == 0, 0.0, halo_ref[...])
    raw = xbc_ref[...]
    conv = cb_ref[...]
    for k in range(SSD_CONV_K):
        conv = conv + cw_ref[k:k + 1, :] * _shift_rows(raw, halo, SSD_CONV_K - 1 - k)
    xbc_all = _silu(conv)
    dtr = dt_ref[...] + dtb_ref[...]
    dt_all = jnp.maximum(dtr, 0.0) + jnp.log(1.0 + jnp.exp(-jnp.abs(dtr)))
    adt_all = dt_all * a_ref[...]
    row = lax.broadcasted_iota(jnp.int32, (lc, lc), 0)
    col = lax.broadcasted_iota(jnp.int32, (lc, lc), 1)
    causal = col <= row
    tril = jnp.where(causal, 1.0, 0.0)
    z_all = z_ref[...]
    dd = d_ref[...]
    s_run = [state[h] for h in range(N_HEADS)]
    for cc in range(SSD_STEP_CHUNKS):
        rs = slice(cc * lc, (cc + 1) * lc)
        xbc = xbc_all[rs]
        dt = dt_all[rs]
        cs = jnp.dot(tril, adt_all[rs], preferred_element_type=F32, precision=HIGHEST)
        cs_t = cs.T
        xs = xbc[:, :GROUP_W]
        for h in range(N_HEADS):
            g = h // (N_HEADS // SSD_GROUPS)
            bm = xbc[:, GROUP_W + g * gn:GROUP_W + (g + 1) * gn]
            cm = xbc[:, GROUP_W + SSD_GROUPS * gn + g * gn:GROUP_W + SSD_GROUPS * gn + (g + 1) * gn]
            cmb = cm.astype(BF16)
            cs_col = cs[:, h:h + 1]
            cs_row = cs_t[h:h + 1, :]
            seg = cs_col - cs_row
            decay = jnp.where(causal, jnp.exp(jnp.where(causal, seg, 0.0)), 0.0)
            scores = (_dot_nt(cmb, bm.astype(BF16)) * decay).astype(BF16)
            hs = slice(h * HEAD_DIM, (h + 1) * HEAD_DIM)
            x_h = xs[:, hs]
            xd = (x_h * dt[:, h:h + 1]).astype(BF16)
            y = _dot(scores, xd)
            s_prev = s_run[h]
            y = y + _dot(cmb, s_prev.astype(BF16)) * jnp.exp(cs_col)
            cs_last = cs[lc - 1:lc, h:h + 1]
            bd = bm * jnp.exp(cs_last - cs_col)
            s_run[h] = jnp.exp(cs_last) * s_prev + lax.dot_general(
                bd.astype(BF16), xd, (((0,), (0,)), ((), ())), preferred_element_type=F32)
            y = y + dd[:, hs] * x_h
            o_ref[rs, hs] = y * _silu(z_all[rs, hs])
    for h in range(N_HEADS):
        state[h] = s_run[h]


def _ssd(z, xbc, dt, conv_w, conv_b, dt_bias, a_log, d_skip, bsz, seq):
    n = bsz * seq
    rows = SSD_STEP_CHUNKS * SSD_CHUNK
    nc = seq // rows
    per = rows // HALO
    dtb = jnp.zeros((1, LANE), F32).at[0, :N_HEADS].set(dt_bias)
    a = jnp.zeros((1, LANE), F32).at[0, :N_HEADS].set(-jnp.exp(a_log))
    dfull = jnp.repeat(d_skip, HEAD_DIM)[None, :]

    def const(shape):
        return pl.BlockSpec(shape, lambda b, c: (0, 0))

    return pl.pallas_call(
        _ssd_kernel,
        grid=(bsz, nc),
        in_specs=[pl.BlockSpec((rows, GROUP_W), lambda b, c: (b * nc + c, 0)),
                  pl.BlockSpec((rows, SSD_CONV_CH), lambda b, c: (b * nc + c, 0)),
                  pl.BlockSpec((HALO, SSD_CONV_CH), lambda b, c: (jnp.maximum((b * nc + c) * per - 1, 0), 0)),
                  pl.BlockSpec((rows, LANE), lambda b, c: (b * nc + c, 0)),
                  const((SSD_CONV_K, SSD_CONV_CH)), const((1, SSD_CONV_CH)), const((1, LANE)),
                  const((1, LANE)), const((1, GROUP_W))],
        out_specs=pl.BlockSpec((rows, GROUP_W), lambda b, c: (b * nc + c, 0)),
        out_shape=jax.ShapeDtypeStruct((n, GROUP_W), F32),
        scratch_shapes=[pltpu.VMEM((N_HEADS, SSD_STATE, HEAD_DIM), F32)],
        compiler_params=_cparams(("parallel", "arbitrary")),
        name="ssd",
    )(z, xbc, xbc, dt, conv_w, conv_b[None, :], dtb, a, dfull)


def _gelu_tanh(x):
    return 0.5 * x * (1.0 + jnp.tanh(math.sqrt(2.0 / math.pi) * (x + 0.044715 * (x * x * x))))


def _group_norm(y, g):
    return (y * lax.rsqrt(jnp.mean(y * y, axis=-1, keepdims=True) + EPS) * g).astype(BF16)


def _merge_kernel(h_ref, ys5a_ref, ys5b_ref, s5u_ref, att_ref, sc_ref, sch_ref, ssd_ref, s5d_ref, wglu_ref,
                  scw_ref, mg_ref, wo_ref, o_ref, *, tm, seq):
    i = pl.program_id(0)
    gw = GROUP_W
    ya = jnp.concatenate([ys5a_ref[...], ys5b_ref[...]], axis=-1) + s5d_ref[...] * s5u_ref[...]
    ga = _gelu_tanh(ya)
    ya = ga * _sigmoid(_dot(ga.astype(BF16), wglu_ref[...]))
    sc = sc_ref[...]
    first = (i * tm) % seq == 0
    halo = jnp.where(first, 0.0, sch_ref[...])
    cx = jnp.concatenate([halo[:, 2 * gw:] * halo[:, :gw], sc[:, 2 * gw:] * sc[:, :gw]], axis=0)
    conv = jnp.zeros((tm, gw), F32)
    for k in range(SCONV_K):
        off = HALO - (SCONV_K - 1) + k
        conv = conv + scw_ref[k:k + 1, :] * cx[off:off + tm, :]
    yc = sc[:, gw:2 * gw] * conv
    mg = mg_ref[...]
    acc = h_ref[...]
    for j, y in enumerate((ya, att_ref[...], yc, ssd_ref[...])):
        acc = acc + _dot(_group_norm(y, mg[:, j * gw:(j + 1) * gw]), wo_ref[j * gw:(j + 1) * gw, :])
    o_ref[...] = acc


def _merge(h, ys5, s5u, att, sc, ssd, s5_d, w_glu, sconv_w, merge_g, w_out, seq, tm=512):
    n = h.shape[0]
    per = tm // HALO

    def tile(wd):
        return pl.BlockSpec((tm, wd), lambda i: (i, 0))

    def const(shape):
        return pl.BlockSpec(shape, lambda i: (0, 0))

    return pl.pallas_call(
        functools.partial(_merge_kernel, tm=tm, seq=seq),
        grid=(n // tm,),
        in_specs=[tile(D_MODEL), tile(LANE), tile(LANE), tile(GROUP_W), tile(GROUP_W), tile(3 * GROUP_W),
                  pl.BlockSpec((HALO, 3 * GROUP_W), lambda i: (jnp.maximum(i * per - 1, 0), 0)),
                  tile(GROUP_W), const((1, GROUP_W)), const((GROUP_W, GROUP_W)),
                  const((SCONV_K, GROUP_W)), const((1, D_MODEL)), const((D_MODEL, D_MODEL))],
        out_specs=tile(D_MODEL),
        out_shape=jax.ShapeDtypeStruct((n, D_MODEL), F32),
        compiler_params=_cparams(("parallel",)),
        name="merge",
    )(h, ys5[0], ys5[1], s5u, att, sc, sc, ssd, s5_d[None, :], w_glu, sconv_w, merge_g[None, :], w_out)


def _ffn_ple_kernel(h_ref, hh_ref, p_ref, g_ref, wg_ref, wu_ref, cw_ref, wd_ref, pg_ref, pwg_ref, pwp_ref,
                    o_ref, *, tm, seq):
    i = pl.program_id(0)
    first = (i * tm) % seq == 0
    g = g_ref[...]
    h = h_ref[...]
    v = _rms(h, g).astype(BF16)
    vh = _rms(hh_ref[...], g).astype(BF16)
    fc = D_FF // FFN_CHUNKS
    acc = h
    for c in range(FFN_CHUNKS):
        cs = slice(c * fc, (c + 1) * fc)
        gate = _dot(v, wg_ref[:, cs])
        gate_h = jnp.where(first, 0.0, _dot(vh, wg_ref[:, cs]))
        ext = jnp.concatenate([gate_h, gate], axis=0)
        conv = jnp.zeros_like(gate)
        for k in range(FFN_CONV_K):
            off = HALO - (FFN_CONV_K - 1) + k
            conv = conv + cw_ref[k:k + 1, cs] * ext[off:off + tm, :]
        act = (_silu(conv) * _dot(v, wu_ref[:, cs])).astype(BF16)
        acc = acc + _dot(act, wd_ref[cs, :])
    gate = _sigmoid(_dot(_rms(acc, pg_ref[...]).astype(BF16), pwg_ref[...]))
    o_ref[...] = acc + gate * _dot(p_ref[...].astype(BF16), pwp_ref[...])


def _ffn_ple(h, p, g, w_gate, w_up, conv_w, w_down, ple_g, ple_w_gate, ple_w_proj, seq, tm=512):
    n = h.shape[0]
    per = tm // HALO

    def resident(shape):
        return pl.BlockSpec(shape, lambda i: (0, 0), pipeline_mode=pl.Buffered(1))

    return pl.pallas_call(
        functools.partial(_ffn_ple_kernel, tm=tm, seq=seq),
        grid=(n // tm,),
        in_specs=[pl.BlockSpec((tm, D_MODEL), lambda i: (i, 0)),
                  pl.BlockSpec((HALO, D_MODEL), lambda i: (jnp.maximum(i * per - 1, 0), 0)),
                  pl.BlockSpec((tm, PLE_DIM), lambda i: (i, 0)),
                  resident((1, D_MODEL)), resident((D_MODEL, D_FF)), resident((D_MODEL, D_FF)),
                  resident((FFN_CONV_K, D_FF)), resident((D_FF, D_MODEL)),
                  resident((1, D_MODEL)), resident((D_MODEL, D_MODEL)), resident((PLE_DIM, D_MODEL))],
        out_specs=pl.BlockSpec((tm, D_MODEL), lambda i: (i, 0)),
        out_shape=jax.ShapeDtypeStruct((n, D_MODEL), F32),
        compiler_params=_cparams(("parallel",)),
        name="ffn_ple",
    )(h, h, p, g, w_gate, w_up, conv_w, w_down, ple_g, ple_w_gate, ple_w_proj)


def kernel(x, p, mix_norm_g, w_in, s5_a_re, s5_a_im, s5_log_dt, s5_b_re, s5_b_im, s5_c_re, s5_c_im, s5_d, s5_w_glu, moba_q_g, moba_k_g, sconv_w, ssd_conv_w, ssd_conv_b, ssd_dt_bias, ssd_a_log, ssd_d, merge_norm_g, w_out, ffn_norm_g, ffn_w_gate, ffn_w_up, ffn_conv_w, ffn_w_down, ple_norm_g, ple_w_gate, ple_w_proj):
    bsz, seq, _ = x.shape
    depth = w_in.shape[0]
    n = bsz * seq
    h = x.reshape(n, D_MODEL)
    w_in_b = jnp.pad(w_in, ((0, 0), (0, 0), (0, PROJ_PAD - w_in.shape[2]))).astype(BF16)
    w_glu_b, w_out_b = s5_w_glu.astype(BF16), w_out.astype(BF16)
    w_gate_b, w_up_b, w_down_b = ffn_w_gate.astype(BF16), ffn_w_up.astype(BF16), ffn_w_down.astype(BF16)
    pw_gate_b, pw_proj_b = ple_w_gate.astype(BF16), ple_w_proj.astype(BF16)
    s5_mats = jax.vmap(_s5_param_mats)(s5_a_re, s5_a_im, s5_log_dt, s5_b_re, s5_b_im, s5_c_re, s5_c_im)
    for i in range(depth):
        s5u, qkv, sc, z, xbc, dt, s5a, s5b = _inproj(h, mix_norm_g[i][None, :], w_in_b[i])
        ys5 = _s5_scan((s5a, s5b), [m[i] for m in s5_mats], bsz, seq)
        att = _moba(qkv, moba_q_g[i][None, :], moba_k_g[i][None, :], bsz, seq)
        yd = _ssd(z, xbc, dt, ssd_conv_w[i], ssd_conv_b[i], ssd_dt_bias[i], ssd_a_log[i], ssd_d[i],
                  bsz, seq)
        h = _merge(h, ys5, s5u, att, sc, yd, s5_d[i], w_glu_b[i], sconv_w[i], merge_norm_g[i], w_out_b[i], seq)
        h = _ffn_ple(h, p[i].reshape(n, PLE_DIM), ffn_norm_g[i][None, :], w_gate_b[i], w_up_b[i],
                     ffn_conv_w[i], w_down_b[i], ple_norm_g[i][None, :], pw_gate_b[i], pw_proj_b[i], seq)
    return h.reshape(bsz, seq, D_MODEL)
```

```python
import functools
import math

import jax
import jax.numpy as jnp
from jax import lax
from jax.experimental import pallas as pl
from jax.experimental.pallas import tpu as pltpu

F32 = jnp.float32
BF16 = jnp.bfloat16
EPS = 1e-6
NEG = -1e30
LOG2E = 1.4426950408889634
HIGHEST = lax.Precision.HIGHEST

D_MODEL = 1024
PLE_DIM = 256
GROUP_W = 256
N_MIXERS = 4
S5_CH = 16
S5_GROUPS = 16
S5_STATE = 64
S5_L = 8
S5_HG = 8
HEAD_DIM = 64
N_HEADS = 4
MOBA_BLOCK = 256
MOBA_TOPK = 3
SSD_GROUPS = 2
SSD_STATE = 128
SSD_CONV_K = 4
SSD_CHUNK = 128
SSD_STEP_CHUNKS = 2
SSD_CONV_CH = GROUP_W + 2 * SSD_GROUPS * SSD_STATE
SCONV_K = 3
D_FF = 2816
FFN_CONV_K = 3
FFN_CHUNKS = 2
HALO = 8
LANE = 128
PROJ_PAD = 2944
VMEM_LIMIT = 56 * 1024 * 1024


def _cparams(sem):
    return pltpu.CompilerParams(dimension_semantics=sem, vmem_limit_bytes=VMEM_LIMIT)


def _rms(x, g):
    return x * lax.rsqrt(jnp.mean(x * x, axis=-1, keepdims=True) + EPS) * g


def _sigmoid(x):
    return 0.5 * jnp.tanh(0.5 * x) + 0.5


def _silu(x):
    return x * _sigmoid(x)


def _dot(a, b):
    return jnp.dot(a, b, preferred_element_type=F32)


def _dot_nt(a, b, precision=None):
    return lax.dot_general(a, b, (((1,), (1,)), ((), ())), preferred_element_type=F32,
                           precision=precision)


def _shift_rows(x, halo, k):
    if k == 0:
        return x
    rolled = pltpu.roll(x, k, axis=0)
    row = lax.broadcasted_iota(jnp.int32, (HALO, x.shape[1]), 0)
    head = jnp.where(row < k, pltpu.roll(halo, k, axis=0), rolled[:HALO])
    return jnp.concatenate([head, rolled[HALO:]], axis=0)


def _causal_conv(x, halo, w_ref, taps, ext_ref):
    rows = x.shape[0]
    ext_ref[0:HALO, :] = halo
    ext_ref[HALO:HALO + rows, :] = x
    acc = w_ref[taps - 1:taps, :] * x
    for k in range(taps - 1):
        off = HALO - (taps - 1 - k)
        acc = acc + w_ref[k:k + 1, :] * ext_ref[off:off + rows, :]
    return acc


def _inproj_kernel(h_ref, hh_ref, g_ref, w_ref, scw_ref, cw_ref, cb_ref,
                   s5u_ref, qkv_ref, yc_ref, zs_ref, xbc_ref, dt_ref, sc_ext, xbc_ext, *, tm, seq):
    first = (pl.program_id(0) * tm) % seq == 0
    g = g_ref[...]
    u = _rms(h_ref[...], g).astype(BF16)
    uh = _rms(hh_ref[...], g).astype(BF16)

    def proj(c0, c1, with_halo=False):
        y = _dot(u, w_ref[:, c0:c1])
        if not with_halo:
            return y
        return y, jnp.where(first, 0.0, _dot(uh, w_ref[:, c0:c1]))

    gw = GROUP_W
    sc, sch = proj(1024, 1792, True)
    z = proj(1792, 2048)
    conv = _causal_conv(sc[:, 2 * gw:] * sc[:, :gw], sch[:, 2 * gw:] * sch[:, :gw], scw_ref, SCONV_K, sc_ext)
    yc_ref[...] = sc[:, gw:2 * gw] * conv
    raw, rawh = proj(2048, 2816, True)
    zs_ref[...] = _silu(z)
    s5u_ref[...] = proj(0, 256)
    qkv_ref[...] = proj(256, 1024)
    dt_ref[...] = proj(2816, PROJ_PAD)
    xbc_ref[...] = _silu(_causal_conv(raw, rawh, cw_ref, SSD_CONV_K, xbc_ext) + cb_ref[...])


def _inproj(h, g, w, sconv_w, ssd_conv_w, ssd_conv_b, seq, tm=512):
    n = h.shape[0]
    per = tm // HALO
    widths = (256, 768, 256, 256, 768, LANE)

    def const(shape):
        return pl.BlockSpec(shape, lambda i: (0, 0))

    return pl.pallas_call(
        functools.partial(_inproj_kernel, tm=tm, seq=seq),
        grid=(n // tm,),
        in_specs=[pl.BlockSpec((tm, D_MODEL), lambda i: (i, 0)),
                  pl.BlockSpec((HALO, D_MODEL), lambda i: (jnp.maximum(i * per - 1, 0), 0)),
                  const((1, D_MODEL)), const((D_MODEL, PROJ_PAD)), const((SCONV_K, GROUP_W)),
                  const((SSD_CONV_K, SSD_CONV_CH)), const((1, SSD_CONV_CH))],
        out_specs=[pl.BlockSpec((tm, wd), lambda i: (i, 0)) for wd in widths],
        out_shape=[jax.ShapeDtypeStruct((n, wd), F32) for wd in widths],
        scratch_shapes=[pltpu.VMEM((HALO + tm, GROUP_W), F32), pltpu.VMEM((HALO + tm, SSD_CONV_CH), F32)],
        compiler_params=_cparams(("parallel",)),
        name="inproj",
    )(h, h, g, w, sconv_w, ssd_conv_w, ssd_conv_b[None, :])


def _s5_param_mats(a_re, a_im, log_dt, b_re, b_im, c_re, c_im):
    L = S5_L
    dt = jnp.exp(log_dt)[:, None]
    taus = jnp.arange(L + 1, dtype=F32)[None, :, None]
    mag = jnp.exp((a_re * dt)[:, None, :] * taus)
    ang = (a_im * dt)[:, None, :] * taus
    pw_r, pw_i = mag * jnp.cos(ang), mag * jnp.sin(ang)
    ab_r, ab_i = pw_r[:, 1], pw_i[:, 1]
    den = a_re * a_re + a_im * a_im
    nr = ab_r - 1.0
    coef_r = (nr * a_re + ab_i * a_im) / den
    coef_i = (ab_i * a_re - nr * a_im) / den
    bb_r = coef_r[..., None] * b_re - coef_i[..., None] * b_im
    bb_i = coef_r[..., None] * b_im + coef_i[..., None] * b_re
    rev_r, rev_i = pw_r[:, L - 1::-1], pw_i[:, L - 1::-1]
    m_r = jnp.einsum("gsp,gpc->gscp", rev_r, bb_r) - jnp.einsum("gsp,gpc->gscp", rev_i, bb_i)
    m_i = jnp.einsum("gsp,gpc->gscp", rev_r, bb_i) + jnp.einsum("gsp,gpc->gscp", rev_i, bb_r)
    cp_r = c_re[:, None] * pw_r[:, :, None, :] - c_im[:, None] * pw_i[:, :, None, :]
    cp_i = c_re[:, None] * pw_i[:, :, None, :] + c_im[:, None] * pw_r[:, :, None, :]
    kk = (jnp.einsum("gtcp,gpd->gtcd", cp_r[:, :L], bb_r, precision=HIGHEST)
          - jnp.einsum("gtcp,gpd->gtcd", cp_i[:, :L], bb_i, precision=HIGHEST))
    s_ix = jnp.arange(L)[:, None]
    l_ix = jnp.arange(L)[None, :]
    lag = l_ix - s_ix
    p_r = cp_r[:, 1:].transpose(0, 3, 1, 2)
    p_i = (-cp_i[:, 1:]).transpose(0, 3, 1, 2)
    hg = S5_HG
    nh = S5_GROUPS // hg
    w = L * hg * S5_CH
    ns = hg * S5_STATE
    g_ix = jnp.arange(hg)[:, None, None]
    place_c = (jnp.arange(LANE)[None, None, :] == g_ix * S5_CH + jnp.arange(S5_CH)[None, :, None]).astype(F32)
    place_p = (jnp.arange(ns)[None, None, :] == g_ix * S5_STATE + jnp.arange(S5_STATE)[None, :, None]).astype(F32)

    def halves(a):
        return a.reshape((nh, hg) + a.shape[1:])

    bd = jnp.einsum("gdi,hgtcd,gcj->htij", place_c, halves(kk), place_c, precision=HIGHEST)
    t4 = jnp.where((lag >= 0)[None, :, :, None, None], bd[:, jnp.clip(lag, 0, L - 1)], 0.0)
    t_mat = t4.transpose(0, 1, 3, 2, 4).reshape(nh, w, w)
    m_mat = jnp.concatenate(
        [jnp.einsum("gdi,hgsdp,gpj->hsij", place_c, halves(m), place_p, precision=HIGHEST).reshape(nh, w, ns)
         for m in (m_r, m_i)], axis=-1)
    p_mat = jnp.concatenate(
        [jnp.einsum("hgplc,gcj->hgplj", halves(q), place_c, precision=HIGHEST).reshape(nh, ns, w)
         for q in (p_r, p_i)], axis=1)
    al_r = pw_r[:, L].reshape(nh, 1, ns)
    al_i = pw_i[:, L].reshape(nh, 1, ns)
    return t_mat.astype(BF16), m_mat.astype(BF16), p_mat.astype(BF16), al_r, al_i


def _s5_kernel(u_ref, t_ref, m_ref, p_ref, alr_ref, ali_ref, y_ref, loc, sp, sr_s, si_s, *, steps, bsz):
    ns = S5_HG * S5_STATE

    @pl.when(pl.program_id(1) == 0)
    def _():
        sr_s[...] = jnp.zeros_like(sr_s)
        si_s[...] = jnp.zeros_like(si_s)

    nt = ns // LANE
    u = jnp.concatenate([u_ref[:, pl.ds(l, steps, stride=S5_L), :].reshape(bsz * steps, LANE)
                         for l in range(S5_L)], axis=1).astype(BF16)
    loc_v = _dot(u, m_ref[0])
    for j in range(2 * nt):
        loc[j] = loc_v[:, j * LANE:(j + 1) * LANE]
    ar = [jnp.broadcast_to(alr_ref[0, :, j * LANE:(j + 1) * LANE], (bsz, LANE)) for j in range(nt)]
    ai = [jnp.broadcast_to(ali_ref[0, :, j * LANE:(j + 1) * LANE], (bsz, LANE)) for j in range(nt)]

    def step(k, carry):
        sr, si = carry
        rows = pl.ds(k, bsz, stride=steps)
        nr, ni = [], []
        for j in range(nt):
            sp[j, rows, :] = sr[j]
            sp[nt + j, rows, :] = si[j]
            nr.append(ar[j] * sr[j] - ai[j] * si[j] + loc[j, rows, :])
            ni.append(ar[j] * si[j] + ai[j] * sr[j] + loc[nt + j, rows, :])
        return tuple(nr), tuple(ni)

    init = (tuple(sr_s[:, j * LANE:(j + 1) * LANE] for j in range(nt)),
            tuple(si_s[:, j * LANE:(j + 1) * LANE] for j in range(nt)))
    sr, si = lax.fori_loop(0, steps, step, init)
    for j in range(nt):
        sr_s[:, j * LANE:(j + 1) * LANE] = sr[j]
        si_s[:, j * LANE:(j + 1) * LANE] = si[j]
    s_prev = jnp.concatenate([sp[j] for j in range(2 * nt)], axis=1).astype(BF16)
    y = _dot(u, t_ref[0]) + _dot(s_prev, p_ref[0])
    for l in range(S5_L):
        y_ref[:, pl.ds(l, steps, stride=S5_L), :] = y[:, l * LANE:(l + 1) * LANE].reshape(bsz, steps, LANE)


def _s5_scan(s5u, mats, bsz, seq, steps=64):
    n_chunks = seq // S5_L
    w = S5_L * LANE
    ns = S5_HG * S5_STATE
    nh = GROUP_W // LANE
    t_mat, m_mat, p_mat, al_r, al_i = mats

    def hspec(a, b):
        return pl.BlockSpec((1, a, b), lambda hf, j: (hf, 0, 0))

    rspec = pl.BlockSpec((bsz, steps * S5_L, LANE), lambda hf, j: (0, j, hf))
    y = pl.pallas_call(
        functools.partial(_s5_kernel, steps=steps, bsz=bsz),
        grid=(nh, n_chunks // steps),
        in_specs=[rspec, hspec(w, w), hspec(w, 2 * ns), hspec(2 * ns, w), hspec(1, ns), hspec(1, ns)],
        out_specs=rspec,
        out_shape=jax.ShapeDtypeStruct((bsz, seq, GROUP_W), F32),
        scratch_shapes=[pltpu.VMEM((2 * ns // LANE, bsz * steps, LANE), F32),
                        pltpu.VMEM((2 * ns // LANE, bsz * steps, LANE), F32),
                        pltpu.VMEM((bsz, ns), F32), pltpu.VMEM((bsz, ns), F32)],
        compiler_params=_cparams(("parallel", "arbitrary")),
        name="s5_scan",
    )(s5u.reshape(bsz, seq, GROUP_W), t_mat, m_mat, p_mat, al_r, al_i)
    return y.reshape(bsz * seq, GROUP_W)


def _moba_kernel(q_ref, k_ref, v_ref, qg_ref, kg_ref, o_ref, kx, vt, kmean, *, n_blocks):
    i = pl.program_id(1)
    blk = MOBA_BLOCK
    hd = HEAD_DIM
    seq = n_blocks * blk

    @pl.when(i == 0)
    def _():
        kg = kg_ref[...]
        lane = lax.broadcasted_iota(jnp.int32, (seq, hd), 1)
        kblk = lax.broadcasted_iota(jnp.int32, (seq, hd), 0) // blk
        onehot = jnp.where(lane == kblk, 1.0, 0.0).astype(BF16)
        for h in range(N_HEADS):
            kh = _rms(k_ref[:, h * hd:(h + 1) * hd], kg)
            kx[h, :, 0:hd] = kh.astype(BF16)
            kx[h, :, hd:2 * hd] = onehot
            kmean[h] = jnp.mean(kh.reshape(n_blocks, blk, hd), axis=1)
        for nb in range(n_blocks):
            vt[:, nb * blk:(nb + 1) * blk] = v_ref[nb * blk:(nb + 1) * blk, :].T.astype(BF16)

    q_t = q_ref[...].T
    qg = qg_ref[...]
    key_ix = lax.broadcasted_iota(jnp.int32, (blk, blk), 0)
    qry_ix = lax.broadcasted_iota(jnp.int32, (blk, blk), 1)
    causal = key_ix <= qry_ix
    bid = lax.broadcasted_iota(jnp.int32, (n_blocks, blk), 0)
    past = bid < i
    own0 = pl.multiple_of(i * blk, blk)

    qxs = []
    for h in range(N_HEADS):
        qh = q_t[h * hd:(h + 1) * hd, :]
        qn = qh * lax.rsqrt(jnp.mean(qh * qh, axis=0, keepdims=True) + EPS) * qg
        gate = jnp.dot(kmean[h], qn, preferred_element_type=F32, precision=HIGHEST)
        gate = jnp.where(past, gate, -jnp.inf)
        cnt = jnp.zeros((n_blocks, blk), F32)
        for m in range(n_blocks):
            gm = gate[m:m + 1, :]
            cnt = cnt + jnp.where(gm > gate, 1.0, jnp.where((gm == gate) & (bid > m), 1.0, 0.0))
        bias = jnp.where((bid == i) | (past & (cnt < MOBA_TOPK)), 0.0, NEG)
        qxs.append(jnp.concatenate([qn * (hd ** -0.5 * LOG2E), bias, jnp.zeros((hd - n_blocks, blk), F32)],
                                   axis=0).astype(BF16))

    def scores(n):
        k0 = pl.multiple_of(n * blk, blk)
        return tuple(_dot(kx[h, pl.ds(k0, blk), :], qxs[h]) for h in range(N_HEADS))

    def values(n, ps):
        k0 = pl.multiple_of(n * blk, blk)
        return [_dot(vt[h * hd:(h + 1) * hd, pl.ds(k0, blk)], ps[h]) for h in range(N_HEADS)]

    s_own = scores(i)
    ps, init = [], []
    for h in range(N_HEADS):
        s = jnp.where(causal, s_own[h], NEG)
        m0 = jnp.max(s, axis=0, keepdims=True)
        p = jnp.exp2(s - m0)
        init.append((m0, jnp.sum(p, axis=0, keepdims=True)))
        ps.append(p.astype(BF16))
    init = tuple((m0, l0, acc0) for (m0, l0), acc0 in zip(init, values(i, ps)))

    def body(n, carry):
        s_cur, stats = carry
        s_next = scores(jnp.minimum(n + 1, jnp.maximum(i - 1, 0)))
        ps, upd = [], []
        for h in range(N_HEADS):
            m_run, l_run, _ = stats[h]
            s = s_cur[h]
            m_new = jnp.maximum(m_run, jnp.max(s, axis=0, keepdims=True))
            alpha = jnp.exp2(m_run - m_new)
            p = jnp.exp2(s - m_new)
            upd.append((m_new, alpha * l_run + jnp.sum(p, axis=0, keepdims=True), alpha))
            ps.append(p.astype(BF16))
        pv = values(n, ps)
        return s_next, tuple((m_new, l_new, alpha * stats[h][2] + pv[h])
                             for h, (m_new, l_new, alpha) in enumerate(upd))

    _, fin = lax.fori_loop(0, i, body, (scores(0), init))
    out_t = jnp.concatenate([acc / l_fin for (_, l_fin, acc) in fin], axis=0)
    o_ref[...] = out_t.T


def _moba(qkv, q_g, k_g, bsz, seq):
    n_blocks = seq // MOBA_BLOCK
    n = bsz * seq
    return pl.pallas_call(
        functools.partial(_moba_kernel, n_blocks=n_blocks),
        grid=(bsz, n_blocks),
        in_specs=[pl.BlockSpec((MOBA_BLOCK, GROUP_W), lambda b, i: (b * n_blocks + i, 0)),
                  pl.BlockSpec((seq, GROUP_W), lambda b, i: (b, 1)),
                  pl.BlockSpec((seq, GROUP_W), lambda b, i: (b, 2)),
                  pl.BlockSpec((HEAD_DIM, 1), lambda b, i: (0, 0)),
                  pl.BlockSpec((1, HEAD_DIM), lambda b, i: (0, 0))],
        out_specs=pl.BlockSpec((MOBA_BLOCK, GROUP_W), lambda b, i: (b * n_blocks + i, 0)),
        out_shape=jax.ShapeDtypeStruct((n, GROUP_W), F32),
        scratch_shapes=[pltpu.VMEM((N_HEADS, seq, 2 * HEAD_DIM), BF16),
                        pltpu.VMEM((GROUP_W, seq), BF16),
                        pltpu.VMEM((N_HEADS, n_blocks, HEAD_DIM), F32)],
        compiler_params=_cparams(("parallel", "arbitrary")),
        name="moba",
    )(qkv, qkv, qkv, q_g.reshape(HEAD_DIM, 1), k_g.reshape(1, HEAD_DIM))


def _ssd_kernel(zs_ref, xbc_ref, dt_ref, dtb_ref, a_ref, d_ref, o_ref, state):
    c = pl.program_id(1)
    lc = SSD_CHUNK
    gn = SSD_STATE

    @pl.when(c == 0)
    def _():
        state[...] = jnp.zeros_like(state)

    xbc_all = xbc_ref[...]
    dtr = dt_ref[...] + dtb_ref[...]
    dt_all = jnp.maximum(dtr, 0.0) + jnp.log(1.0 + jnp.exp(-jnp.abs(dtr)))
    adt_all = dt_all * a_ref[...]
    row = lax.broadcasted_iota(jnp.int32, (lc, lc), 0)
    col = lax.broadcasted_iota(jnp.int32, (lc, lc), 1)
    causal = col <= row
    tril = jnp.where(causal, 1.0, 0.0)
    zs_all = zs_ref[...]
    dd = d_ref[...]
    s_run = [state[h] for h in range(N_HEADS)]
    for cc in range(SSD_STEP_CHUNKS):
        rs = slice(cc * lc, (cc + 1) * lc)
        xbc = xbc_all[rs]
        dt = dt_all[rs]
        cs = jnp.dot(tril, adt_all[rs], preferred_element_type=F32, precision=HIGHEST)
        cs_t = cs.T
        xs = xbc[:, :GROUP_W]
        for h in range(N_HEADS):
            g = h // (N_HEADS // SSD_GROUPS)
            bm = xbc[:, GROUP_W + g * gn:GROUP_W + (g + 1) * gn]
            cm = xbc[:, GROUP_W + SSD_GROUPS * gn + g * gn:GROUP_W + SSD_GROUPS * gn + (g + 1) * gn]
            cmb = cm.astype(BF16)
            cs_b = jnp.broadcast_to(cs[:, h:h + 1], (lc, lc))
            cs_row = cs_t[h:h + 1, :]
            seg = cs_b - cs_row
            decay = jnp.where(causal, jnp.exp(jnp.where(causal, seg, 0.0)), 0.0)
            scores = (_dot_nt(cmb, bm.astype(BF16)) * decay).astype(BF16)
            hs = slice(h * HEAD_DIM, (h + 1) * HEAD_DIM)
            x_h = xs[:, hs]
            xd = (x_h * jnp.broadcast_to(dt[:, h:h + 1], (lc, HEAD_DIM))).astype(BF16)
            y = _dot(scores, xd)
            s_prev = s_run[h]
            y = y + _dot(cmb, s_prev.astype(BF16)) * jnp.exp(cs_b[:, :HEAD_DIM])
            cs_last = cs_b[lc - 1:lc, :]
            bd = bm * jnp.exp(cs_last - cs_b)
            s_run[h] = jnp.exp(cs_last[:, :HEAD_DIM]) * s_prev + lax.dot_general(
                bd.astype(BF16), xd, (((0,), (0,)), ((), ())), preferred_element_type=F32)
            y = y + dd[:, hs] * x_h
            o_ref[rs, hs] = y * zs_all[rs, hs]
    for h in range(N_HEADS):
        state[h] = s_run[h]


def _ssd(zs, xbc, dt, dt_bias, a_log, d_skip, bsz, seq):
    n = bsz * seq
    rows = SSD_STEP_CHUNKS * SSD_CHUNK
    nc = seq // rows
    dtb = jnp.zeros((1, LANE), F32).at[0, :N_HEADS].set(dt_bias)
    a = jnp.zeros((1, LANE), F32).at[0, :N_HEADS].set(-jnp.exp(a_log))
    dfull = jnp.repeat(d_skip, HEAD_DIM)[None, :]

    def const(shape):
        return pl.BlockSpec(shape, lambda b, c: (0, 0))

    return pl.pallas_call(
        _ssd_kernel,
        grid=(bsz, nc),
        in_specs=[pl.BlockSpec((rows, GROUP_W), lambda b, c: (b * nc + c, 0)),
                  pl.BlockSpec((rows, SSD_CONV_CH), lambda b, c: (b * nc + c, 0)),
                  pl.BlockSpec((rows, LANE), lambda b, c: (b * nc + c, 0)),
                  const((1, LANE)), const((1, LANE)), const((1, GROUP_W))],
        out_specs=pl.BlockSpec((rows, GROUP_W), lambda b, c: (b * nc + c, 0)),
        out_shape=jax.ShapeDtypeStruct((n, GROUP_W), F32),
        scratch_shapes=[pltpu.VMEM((N_HEADS, SSD_STATE, HEAD_DIM), F32)],
        compiler_params=_cparams(("parallel", "arbitrary")),
        name="ssd",
    )(zs, xbc, dt, dtb, a, dfull)


def _gelu_tanh(x):
    return 0.5 * x * (1.0 + jnp.tanh(math.sqrt(2.0 / math.pi) * (x + 0.044715 * (x * x * x))))


def _group_norm(y, g):
    return (y * lax.rsqrt(jnp.mean(y * y, axis=-1, keepdims=True) + EPS) * g).astype(BF16)


def _merge_rows(h, ys5, s5u, att, yc, ssd, s5d, wglu_ref, mg, wo_ref):
    gw = GROUP_W
    ya = ys5 + s5d * s5u
    ga = _gelu_tanh(ya)
    ya = ga * _sigmoid(_dot(ga.astype(BF16), wglu_ref[...]))
    acc = h
    for j, y in enumerate((ya, att, yc, ssd)):
        acc = acc + _dot(_group_norm(y, mg[:, j * gw:(j + 1) * gw]), wo_ref[j * gw:(j + 1) * gw, :])
    return acc


def _tail_kernel(*refs, tm, seq):
    main = refs[0:6]
    halo = refs[6:12]
    (p_ref, s5d_ref, wglu_ref, mg_ref, wo_ref, g_ref, wg_ref, wu_ref, cw_ref, wd_ref,
     pg_ref, pwg_ref, pwp_ref, o_ref) = refs[12:]
    first = (pl.program_id(0) * tm) % seq == 0
    s5d = s5d_ref[...]
    mg = mg_ref[...]
    h = _merge_rows(*[r[...] for r in main], s5d, wglu_ref, mg, wo_ref)
    hh = _merge_rows(*[r[...] for r in halo], s5d, wglu_ref, mg, wo_ref)
    g = g_ref[...]
    v = _rms(h, g).astype(BF16)
    vh = _rms(hh, g).astype(BF16)
    fc = D_FF // FFN_CHUNKS
    ffn = None
    for c in range(FFN_CHUNKS):
        cs = slice(c * fc, (c + 1) * fc)
        gate = _dot(v, wg_ref[:, cs])
        gate_h = jnp.where(first, 0.0, _dot(vh, wg_ref[:, cs]))
        conv = cw_ref[FFN_CONV_K - 1:FFN_CONV_K, cs] * gate
        for k in range(FFN_CONV_K - 1):
            conv = conv + cw_ref[k:k + 1, cs] * _shift_rows(gate, gate_h, FFN_CONV_K - 1 - k)
        act = (_silu(conv) * _dot(v, wu_ref[:, cs])).astype(BF16)
        down = _dot(act, wd_ref[cs, :])
        ffn = down if ffn is None else ffn + down
    acc = h + ffn
    gate = _sigmoid(_dot(_rms(acc, pg_ref[...]).astype(BF16), pwg_ref[...]))
    o_ref[...] = acc + gate * _dot(p_ref[...].astype(BF16), pwp_ref[...])


def _tail(h, ys5, s5u, att, yc, ssd, p, layer, s5_d, w_glu, merge_g, w_out, ffn_g, w_gate, w_up, conv_w, w_down,
          ple_g, ple_w_gate, ple_w_proj, seq, tm=512):
    n = h.shape[0]
    per = tm // HALO
    widths = (D_MODEL,) + (GROUP_W,) * 5

    def resident(shape):
        return pl.BlockSpec(shape, lambda i: (0, 0), pipeline_mode=pl.Buffered(1))

    return pl.pallas_call(
        functools.partial(_tail_kernel, tm=tm, seq=seq),
        grid=(n // tm,),
        in_specs=([pl.BlockSpec((tm, wd), lambda i: (i, 0)) for wd in widths]
                  + [pl.BlockSpec((HALO, wd), lambda i: (jnp.maximum(i * per - 1, 0), 0)) for wd in widths]
                  + [pl.BlockSpec((None, tm, PLE_DIM), lambda i: (layer, i, 0)),
                     resident((1, GROUP_W)), resident((GROUP_W, GROUP_W)), resident((1, D_MODEL)),
                     resident((D_MODEL, D_MODEL)), resident((1, D_MODEL)), resident((D_MODEL, D_FF)),
                     resident((D_MODEL, D_FF)), resident((FFN_CONV_K, D_FF)), resident((D_FF, D_MODEL)),
                     resident((1, D_MODEL)), resident((D_MODEL, D_MODEL)), resident((PLE_DIM, D_MODEL))]),
        out_specs=pl.BlockSpec((tm, D_MODEL), lambda i: (i, 0)),
        out_shape=jax.ShapeDtypeStruct((n, D_MODEL), F32),
        compiler_params=_cparams(("parallel",)),
        name="tail",
    )(h, ys5, s5u, att, yc, ssd, h, ys5, s5u, att, yc, ssd, p,
      s5_d[None, :], w_glu, merge_g[None, :], w_out, ffn_g[None, :], w_gate, w_up, conv_w, w_down,
      ple_g[None, :], ple_w_gate, ple_w_proj)


def kernel(x, p, mix_norm_g, w_in, s5_a_re, s5_a_im, s5_log_dt, s5_b_re, s5_b_im, s5_c_re, s5_c_im, s5_d, s5_w_glu, moba_q_g, moba_k_g, sconv_w, ssd_conv_w, ssd_conv_b, ssd_dt_bias, ssd_a_log, ssd_d, merge_norm_g, w_out, ffn_norm_g, ffn_w_gate, ffn_w_up, ffn_conv_w, ffn_w_down, ple_norm_g, ple_w_gate, ple_w_proj):
    bsz, seq, _ = x.shape
    depth = w_in.shape[0]
    n = bsz * seq
    h = x.reshape(n, D_MODEL)
    w_in_b = jnp.pad(w_in, ((0, 0), (0, 0), (0, PROJ_PAD - w_in.shape[2]))).astype(BF16)
    w_glu_b, w_out_b = s5_w_glu.astype(BF16), w_out.astype(BF16)
    w_gate_b, w_up_b, w_down_b = ffn_w_gate.astype(BF16), ffn_w_up.astype(BF16), ffn_w_down.astype(BF16)
    pw_gate_b, pw_proj_b = ple_w_gate.astype(BF16), ple_w_proj.astype(BF16)
    s5_mats = jax.vmap(_s5_param_mats)(s5_a_re, s5_a_im, s5_log_dt, s5_b_re, s5_b_im, s5_c_re, s5_c_im)
    p_rows = p.reshape(depth, n, PLE_DIM)
    for i in range(depth):
        s5u, qkv, yc, zs, xbc, dt = _inproj(h, mix_norm_g[i][None, :], w_in_b[i], sconv_w[i],
                                            ssd_conv_w[i], ssd_conv_b[i], seq)
        ys5 = _s5_scan(s5u, [m[i] for m in s5_mats], bsz, seq)
        att = _moba(qkv, moba_q_g[i][None, :], moba_k_g[i][None, :], bsz, seq)
        yd = _ssd(zs, xbc, dt, ssd_dt_bias[i], ssd_a_log[i], ssd_d[i], bsz, seq)
        h = _tail(h, ys5, s5u, att, yc, yd, p_rows, i, s5_d[i], w_glu_b[i], merge_norm_g[i], w_out_b[i],
                  ffn_norm_g[i], w_gate_b[i], w_up_b[i], ffn_conv_w[i], w_down_b[i],
                  ple_norm_g[i], pw_gate_b[i], pw_proj_b[i], seq)
    return h.reshape(bsz, seq, D_MODEL)
```

```python
import functools
import math

import jax
import jax.numpy as jnp
from jax import lax
from jax.experimental import pallas as pl
from jax.experimental.pallas import tpu as pltpu

F32 = jnp.float32
BF16 = jnp.bfloat16
EPS = 1e-6
NEG = -1e30
LOG2E = 1.4426950408889634
HIGHEST = lax.Precision.HIGHEST

D_MODEL = 1024
PLE_DIM = 256
GROUP_W = 256
N_MIXERS = 4
S5_CH = 16
S5_GROUPS = 16
S5_STATE = 64
S5_L = 8
S5_HG = 8
HEAD_DIM = 64
N_HEADS = 4
MOBA_BLOCK = 256
MOBA_TOPK = 3
SSD_GROUPS = 2
SSD_STATE = 128
SSD_CONV_K = 4
SSD_CHUNK = 128
SSD_STEP_CHUNKS = 2
SSD_CONV_CH = GROUP_W + 2 * SSD_GROUPS * SSD_STATE
SCONV_K = 3
D_FF = 2816
FFN_CONV_K = 3
FFN_CHUNKS = 2
HALO = 8
ROW_SPLIT = 1
LANE = 128
PROJ_PAD = 2944
VMEM_LIMIT = 56 * 1024 * 1024


def _cparams(sem):
    return pltpu.CompilerParams(dimension_semantics=sem, vmem_limit_bytes=VMEM_LIMIT)


def _rms(x, g):
    return x * lax.rsqrt(jnp.mean(x * x, axis=-1, keepdims=True) + EPS) * g


def _sigmoid(x):
    return 0.5 * jnp.tanh(0.5 * x) + 0.5


def _silu(x):
    return x * _sigmoid(x)


def _dot(a, b):
    return jnp.dot(a, b, preferred_element_type=F32)


def _dot_nt(a, b, precision=None):
    return lax.dot_general(a, b, (((1,), (1,)), ((), ())), preferred_element_type=F32,
                           precision=precision)


def _shift_rows(x, halo, k):
    if k == 0:
        return x
    rolled = pltpu.roll(x, k, axis=0)
    row = lax.broadcasted_iota(jnp.int32, (HALO, x.shape[1]), 0)
    head = jnp.where(row < k, pltpu.roll(halo, k, axis=0), rolled[:HALO])
    return jnp.concatenate([head, rolled[HALO:]], axis=0)


def _causal_conv(x, halo, w_ref, taps, cols=slice(None)):
    acc = w_ref[taps - 1:taps, cols] * x
    for k in range(taps - 1):
        acc = acc + w_ref[k:k + 1, cols] * _shift_rows(x, halo, taps - 1 - k)
    return acc


def _inproj_kernel(h_ref, hh_ref, g_ref, w_ref, scw_ref, cw_ref, cb_ref,
                   s5u_ref, qkv_ref, yc_ref, zs_ref, xbc_ref, dt_ref, *, tm, seq):
    first = (pl.program_id(0) * tm) % seq == 0
    g = g_ref[...]
    rm = tm // ROW_SPLIT
    rows = [slice(a * rm, (a + 1) * rm) for a in range(ROW_SPLIT)]
    u = [_rms(h_ref[r, :], g).astype(BF16) for r in rows]
    uh = _rms(hh_ref[...], g).astype(BF16)

    def proj(c0, c1):
        return [_dot(ua, w_ref[:, c0:c1]) for ua in u]

    def halo_proj(c0, c1):
        return jnp.where(first, 0.0, _dot(uh, w_ref[:, c0:c1]))

    gw = GROUP_W
    sc = proj(1024, 1792)
    sch = halo_proj(1024, 1792)
    z = proj(1792, 2048)
    cx_halo = sch[:, 2 * gw:] * sch[:, :gw]
    for a, r in enumerate(rows):
        cx = sc[a][:, 2 * gw:] * sc[a][:, :gw]
        yc_ref[r, :] = sc[a][:, gw:2 * gw] * _causal_conv(cx, cx_halo, scw_ref, SCONV_K)
        cx_halo = cx[rm - HALO:, :]
    raw = proj(2048, 2816)
    raw_halo = halo_proj(2048, 2816)
    for a, r in enumerate(rows):
        zs_ref[r, :] = _silu(z[a])
    for ref, (c0, c1) in ((s5u_ref, (0, 256)), (qkv_ref, (256, 1024)), (dt_ref, (2816, PROJ_PAD))):
        for r, y in zip(rows, proj(c0, c1)):
            ref[r, :] = y
    for a, r in enumerate(rows):
        xbc_ref[r, :] = _silu(_causal_conv(raw[a], raw_halo, cw_ref, SSD_CONV_K) + cb_ref[...])
        raw_halo = raw[a][rm - HALO:, :]


def _inproj(h, g, w, sconv_w, ssd_conv_w, ssd_conv_b, seq, tm=512):
    n = h.shape[0]
    per = tm // HALO
    widths = (256, 768, 256, 256, 768, LANE)

    def const(shape):
        return pl.BlockSpec(shape, lambda i: (0, 0))

    return pl.pallas_call(
        functools.partial(_inproj_kernel, tm=tm, seq=seq),
        grid=(n // tm,),
        in_specs=[pl.BlockSpec((tm, D_MODEL), lambda i: (i, 0)),
                  pl.BlockSpec((HALO, D_MODEL), lambda i: (jnp.maximum(i * per - 1, 0), 0)),
                  const((1, D_MODEL)), const((D_MODEL, PROJ_PAD)), const((SCONV_K, GROUP_W)),
                  const((SSD_CONV_K, SSD_CONV_CH)), const((1, SSD_CONV_CH))],
        out_specs=[pl.BlockSpec((tm, wd), lambda i: (i, 0)) for wd in widths],
        out_shape=[jax.ShapeDtypeStruct((n, wd), F32) for wd in widths],
        compiler_params=_cparams(("parallel",)),
        name="inproj",
    )(h, h, g, w, sconv_w, ssd_conv_w, ssd_conv_b[None, :])


def _s5_param_mats(a_re, a_im, log_dt, b_re, b_im, c_re, c_im):
    L = S5_L
    dt = jnp.exp(log_dt)[:, None]
    taus = jnp.arange(L + 1, dtype=F32)[None, :, None]
    mag = jnp.exp((a_re * dt)[:, None, :] * taus)
    ang = (a_im * dt)[:, None, :] * taus
    pw_r, pw_i = mag * jnp.cos(ang), mag * jnp.sin(ang)
    ab_r, ab_i = pw_r[:, 1], pw_i[:, 1]
    den = a_re * a_re + a_im * a_im
    nr = ab_r - 1.0
    coef_r = (nr * a_re + ab_i * a_im) / den
    coef_i = (ab_i * a_re - nr * a_im) / den
    bb_r = coef_r[..., None] * b_re - coef_i[..., None] * b_im
    bb_i = coef_r[..., None] * b_im + coef_i[..., None] * b_re
    rev_r, rev_i = pw_r[:, L - 1::-1], pw_i[:, L - 1::-1]
    m_r = jnp.einsum("gsp,gpc->gscp", rev_r, bb_r) - jnp.einsum("gsp,gpc->gscp", rev_i, bb_i)
    m_i = jnp.einsum("gsp,gpc->gscp", rev_r, bb_i) + jnp.einsum("gsp,gpc->gscp", rev_i, bb_r)
    cp_r = c_re[:, None] * pw_r[:, :, None, :] - c_im[:, None] * pw_i[:, :, None, :]
    cp_i = c_re[:, None] * pw_i[:, :, None, :] + c_im[:, None] * pw_r[:, :, None, :]
    kk = (jnp.einsum("gtcp,gpd->gtcd", cp_r[:, :L], bb_r, precision=HIGHEST)
          - jnp.einsum("gtcp,gpd->gtcd", cp_i[:, :L], bb_i, precision=HIGHEST))
    s_ix = jnp.arange(L)[:, None]
    l_ix = jnp.arange(L)[None, :]
    lag = l_ix - s_ix
    p_r = cp_r[:, 1:].transpose(0, 3, 1, 2)
    p_i = (-cp_i[:, 1:]).transpose(0, 3, 1, 2)
    hg = S5_HG
    nh = S5_GROUPS // hg
    w = L * hg * S5_CH
    ns = hg * S5_STATE
    g_ix = jnp.arange(hg)[:, None, None]
    place_c = (jnp.arange(LANE)[None, None, :] == g_ix * S5_CH + jnp.arange(S5_CH)[None, :, None]).astype(F32)
    place_p = (jnp.arange(ns)[None, None, :] == g_ix * S5_STATE + jnp.arange(S5_STATE)[None, :, None]).astype(F32)

    def halves(a):
        return a.reshape((nh, hg) + a.shape[1:])

    bd = jnp.einsum("gdi,hgtcd,gcj->htij", place_c, halves(kk), place_c, precision=HIGHEST)
    t4 = jnp.where((lag >= 0)[None, :, :, None, None], bd[:, jnp.clip(lag, 0, L - 1)], 0.0)
    t_mat = t4.transpose(0, 1, 3, 2, 4).reshape(nh, w, w)
    m_mat = jnp.concatenate(
        [jnp.einsum("gdi,hgsdp,gpj->hsij", place_c, halves(m), place_p, precision=HIGHEST).reshape(nh, w, ns)
         for m in (m_r, m_i)], axis=-1)
    p_mat = jnp.concatenate(
        [jnp.einsum("hgplc,gcj->hgplj", halves(q), place_c, precision=HIGHEST).reshape(nh, ns, w)
         for q in (p_r, p_i)], axis=1)
    al_r = pw_r[:, L].reshape(nh, 1, ns)
    al_i = pw_i[:, L].reshape(nh, 1, ns)
    return t_mat.astype(BF16), m_mat.astype(BF16), p_mat.astype(BF16), al_r, al_i


def _s5_kernel(u_ref, t_ref, m_ref, p_ref, alr_ref, ali_ref, y_ref, loc, sp, sr_s, si_s, *, steps, bsz):
    ns = S5_HG * S5_STATE

    @pl.when(pl.program_id(1) == 0)
    def _():
        sr_s[...] = jnp.zeros_like(sr_s)
        si_s[...] = jnp.zeros_like(si_s)

    nt = ns // LANE
    u = jnp.concatenate([u_ref[:, pl.ds(l, steps, stride=S5_L), :].reshape(bsz * steps, LANE)
                         for l in range(S5_L)], axis=1).astype(BF16)
    loc_v = _dot(u, m_ref[0])
    for j in range(2 * nt):
        loc[j] = loc_v[:, j * LANE:(j + 1) * LANE]
    ar = [jnp.broadcast_to(alr_ref[0, :, j * LANE:(j + 1) * LANE], (bsz, LANE)) for j in range(nt)]
    ai = [jnp.broadcast_to(ali_ref[0, :, j * LANE:(j + 1) * LANE], (bsz, LANE)) for j in range(nt)]

    def step(k, carry):
        sr, si = carry
        rows = pl.ds(k, bsz, stride=steps)
        nr, ni = [], []
        for j in range(nt):
            sp[j, rows, :] = sr[j]
            sp[nt + j, rows, :] = si[j]
            nr.append(ar[j] * sr[j] - ai[j] * si[j] + loc[j, rows, :])
            ni.append(ar[j] * si[j] + ai[j] * sr[j] + loc[nt + j, rows, :])
        return tuple(nr), tuple(ni)

    init = (tuple(sr_s[:, j * LANE:(j + 1) * LANE] for j in range(nt)),
            tuple(si_s[:, j * LANE:(j + 1) * LANE] for j in range(nt)))
    sr, si = lax.fori_loop(0, steps, step, init)
    for j in range(nt):
        sr_s[:, j * LANE:(j + 1) * LANE] = sr[j]
        si_s[:, j * LANE:(j + 1) * LANE] = si[j]
    s_prev = jnp.concatenate([sp[j] for j in range(2 * nt)], axis=1).astype(BF16)
    y = _dot(u, t_ref[0]) + _dot(s_prev, p_ref[0])
    for l in range(S5_L):
        y_ref[:, pl.ds(l, steps, stride=S5_L), :] = y[:, l * LANE:(l + 1) * LANE].reshape(bsz, steps, LANE)


def _s5_scan(s5u, mats, bsz, seq, steps=64):
    n_chunks = seq // S5_L
    w = S5_L * LANE
    ns = S5_HG * S5_STATE
    nh = GROUP_W // LANE
    t_mat, m_mat, p_mat, al_r, al_i = mats

    def hspec(a, b):
        return pl.BlockSpec((1, a, b), lambda hf, j: (hf, 0, 0))

    rspec = pl.BlockSpec((bsz, steps * S5_L, LANE), lambda hf, j: (0, j, hf))
    y = pl.pallas_call(
        functools.partial(_s5_kernel, steps=steps, bsz=bsz),
        grid=(nh, n_chunks // steps),
        in_specs=[rspec, hspec(w, w), hspec(w, 2 * ns), hspec(2 * ns, w), hspec(1, ns), hspec(1, ns)],
        out_specs=rspec,
        out_shape=jax.ShapeDtypeStruct((bsz, seq, GROUP_W), F32),
        scratch_shapes=[pltpu.VMEM((2 * ns // LANE, bsz * steps, LANE), F32),
                        pltpu.VMEM((2 * ns // LANE, bsz * steps, LANE), F32),
                        pltpu.VMEM((bsz, ns), F32), pltpu.VMEM((bsz, ns), F32)],
        compiler_params=_cparams(("parallel", "arbitrary")),
        name="s5_scan",
    )(s5u.reshape(bsz, seq, GROUP_W), t_mat, m_mat, p_mat, al_r, al_i)
    return y.reshape(bsz * seq, GROUP_W)


def _moba_kernel(q_ref, k_ref, v_ref, qg_ref, kg_ref, o_ref, kx, vt, kmean, s_a, s_b, acc_s, m_s, l_s, *, n_blocks):
    i = pl.program_id(1)
    blk = MOBA_BLOCK
    hd = HEAD_DIM
    seq = n_blocks * blk

    @pl.when(i == 0)
    def _():
        kg = kg_ref[...]
        lane = lax.broadcasted_iota(jnp.int32, (seq, hd), 1)
        kblk = lax.broadcasted_iota(jnp.int32, (seq, hd), 0) // blk
        onehot = jnp.where(lane == kblk, 1.0, 0.0).astype(BF16)
        for h in range(N_HEADS):
            kh = _rms(k_ref[:, h * hd:(h + 1) * hd], kg)
            kx[h, :, 0:hd] = kh.astype(BF16)
            kx[h, :, hd:2 * hd] = onehot
            kmean[h] = jnp.mean(kh.reshape(n_blocks, blk, hd), axis=1)
        for nb in range(n_blocks):
            vt[:, nb * blk:(nb + 1) * blk] = v_ref[nb * blk:(nb + 1) * blk, :].T.astype(BF16)

    q_t = q_ref[...].T
    qg = qg_ref[...]
    key_ix = lax.broadcasted_iota(jnp.int32, (blk, blk), 0)
    qry_ix = lax.broadcasted_iota(jnp.int32, (blk, blk), 1)
    causal = key_ix <= qry_ix
    bid = lax.broadcasted_iota(jnp.int32, (n_blocks, blk), 0)
    past = bid < i
    own0 = pl.multiple_of(i * blk, blk)

    qns, qss, s_own = [], [], []
    for h in range(N_HEADS):
        qh = q_t[h * hd:(h + 1) * hd, :]
        qn = qh * lax.rsqrt(jnp.mean(qh * qh, axis=0, keepdims=True) + EPS) * qg
        qs = qn * (hd ** -0.5 * LOG2E)
        qns.append(qn)
        qss.append(qs)
        s_own.append(_dot(kx[h, pl.ds(own0, blk), :],
                          jnp.concatenate([qs, jnp.zeros((hd, blk), F32)], axis=0).astype(BF16)))

    qxs = []
    for h in range(N_HEADS):
        qn = qns[h]
        gate = jnp.dot(kmean[h], qn, preferred_element_type=F32, precision=HIGHEST)
        gate = jnp.where(past, gate, -jnp.inf)
        cnt = jnp.zeros((n_blocks, blk), F32)
        for m in range(n_blocks):
            gm = gate[m:m + 1, :]
            cnt = cnt + jnp.where(gm > gate, 1.0, jnp.where((gm == gate) & (bid > m), 1.0, 0.0))
        bias = jnp.where((bid == i) | (past & (cnt < MOBA_TOPK)), 0.0, NEG)
        qxs.append(jnp.concatenate([qss[h], bias, jnp.zeros((hd - n_blocks, blk), F32)],
                                   axis=0).astype(BF16))

    heads = range(N_HEADS)

    def issue_scores(n, dst):
        k0 = pl.multiple_of(n * blk, blk)
        for h in heads:
            dst[h] = _dot(kx[h, pl.ds(k0, blk), :], qxs[h])

    def values(n, ps):
        k0 = pl.multiple_of(n * blk, blk)
        return [_dot(vt[h * hd:(h + 1) * hd, pl.ds(k0, blk)], ps[h]) for h in heads]

    ps = []
    for h in heads:
        s = jnp.where(causal, s_own[h], NEG)
        m0 = jnp.max(s, axis=0, keepdims=True)
        p = jnp.exp2(s - m0)
        m_s[h] = m0
        l_s[h] = jnp.sum(p, axis=0, keepdims=True)
        ps.append(p.astype(BF16))
    for h, pv in zip(heads, values(i, ps)):
        acc_s[h] = pv

    def step(n, src, dst):
        issue_scores(jnp.minimum(n + 1, jnp.maximum(i - 1, 0)), dst)
        ps, alphas = [], []
        for h in heads:
            m_run = m_s[h]
            s = src[h]
            m_new = jnp.maximum(m_run, jnp.max(s, axis=0, keepdims=True))
            alpha = jnp.exp2(m_run - m_new)
            p = jnp.exp2(s - m_new)
            m_s[h] = m_new
            l_s[h] = alpha * l_s[h] + jnp.sum(p, axis=0, keepdims=True)
            alphas.append(alpha)
            ps.append(p.astype(BF16))
        for h, pv in zip(heads, values(n, ps)):
            acc_s[h] = alphas[h] * acc_s[h] + pv

    issue_scores(0, s_a)

    def pair(k, carry):
        step(2 * k, s_a, s_b)
        step(2 * k + 1, s_b, s_a)
        return carry

    lax.fori_loop(0, i // 2, pair, 0)

    @pl.when(i % 2 == 1)
    def _():
        step(i - 1, s_a, s_b)

    o_ref[...] = jnp.concatenate([acc_s[h] / l_s[h] for h in heads], axis=0).T


def _moba(qkv, q_g, k_g, bsz, seq):
    n_blocks = seq // MOBA_BLOCK
    n = bsz * seq
    return pl.pallas_call(
        functools.partial(_moba_kernel, n_blocks=n_blocks),
        grid=(bsz, n_blocks),
        in_specs=[pl.BlockSpec((MOBA_BLOCK, GROUP_W), lambda b, i: (b * n_blocks + i, 0)),
                  pl.BlockSpec((seq, GROUP_W), lambda b, i: (b, 1)),
                  pl.BlockSpec((seq, GROUP_W), lambda b, i: (b, 2)),
                  pl.BlockSpec((HEAD_DIM, 1), lambda b, i: (0, 0)),
                  pl.BlockSpec((1, HEAD_DIM), lambda b, i: (0, 0))],
        out_specs=pl.BlockSpec((MOBA_BLOCK, GROUP_W), lambda b, i: (b * n_blocks + i, 0)),
        out_shape=jax.ShapeDtypeStruct((n, GROUP_W), F32),
        scratch_shapes=[pltpu.VMEM((N_HEADS, seq, 2 * HEAD_DIM), BF16),
                        pltpu.VMEM((GROUP_W, seq), BF16),
                        pltpu.VMEM((N_HEADS, n_blocks, HEAD_DIM), F32),
                        pltpu.VMEM((N_HEADS, MOBA_BLOCK, MOBA_BLOCK), F32),
                        pltpu.VMEM((N_HEADS, MOBA_BLOCK, MOBA_BLOCK), F32),
                        pltpu.VMEM((N_HEADS, HEAD_DIM, MOBA_BLOCK), F32),
                        pltpu.VMEM((N_HEADS, 1, MOBA_BLOCK), F32),
                        pltpu.VMEM((N_HEADS, 1, MOBA_BLOCK), F32)],
        compiler_params=_cparams(("parallel", "arbitrary")),
        name="moba",
    )(qkv, qkv, qkv, q_g.reshape(HEAD_DIM, 1), k_g.reshape(1, HEAD_DIM))


def _ssd_kernel(zs_ref, xbc_ref, dt_ref, dtb_ref, a_ref, d_ref, o_ref, state):
    c = pl.program_id(1)
    lc = SSD_CHUNK
    gn = SSD_STATE

    @pl.when(c == 0)
    def _():
        state[...] = jnp.zeros_like(state)

    xbc_all = xbc_ref[...]
    dtr = dt_ref[...] + dtb_ref[...]
    dt_all = jnp.maximum(dtr, 0.0) + jnp.log(1.0 + jnp.exp(-jnp.abs(dtr)))
    adt_all = dt_all * a_ref[...]
    row = lax.broadcasted_iota(jnp.int32, (lc, lc), 0)
    col = lax.broadcasted_iota(jnp.int32, (lc, lc), 1)
    causal = col <= row
    tril = jnp.where(causal, 1.0, 0.0)
    zs_all = zs_ref[...]
    dd = d_ref[...]
    s_run = [state[h] for h in range(N_HEADS)]
    for cc in range(SSD_STEP_CHUNKS):
        rs = slice(cc * lc, (cc + 1) * lc)
        xbc = xbc_all[rs]
        dt = dt_all[rs]
        cs = jnp.dot(tril, adt_all[rs], preferred_element_type=F32, precision=HIGHEST)
        cs_t = cs.T
        xs = xbc[:, :GROUP_W]
        for h in range(N_HEADS):
            g = h // (N_HEADS // SSD_GROUPS)
            bm = xbc[:, GROUP_W + g * gn:GROUP_W + (g + 1) * gn]
            cm = xbc[:, GROUP_W + SSD_GROUPS * gn + g * gn:GROUP_W + SSD_GROUPS * gn + (g + 1) * gn]
            cmb = cm.astype(BF16)
            cs_b = jnp.broadcast_to(cs[:, h:h + 1], (lc, lc))
            cs_row = cs_t[h:h + 1, :]
            seg = cs_b - cs_row
            decay = jnp.where(causal, jnp.exp(jnp.where(causal, seg, 0.0)), 0.0)
            scores = (_dot_nt(cmb, bm.astype(BF16)) * decay).astype(BF16)
            hs = slice(h * HEAD_DIM, (h + 1) * HEAD_DIM)
            x_h = xs[:, hs]
            xd = (x_h * jnp.broadcast_to(dt[:, h:h + 1], (lc, HEAD_DIM))).astype(BF16)
            y = _dot(scores, xd)
            s_prev = s_run[h]
            y = y + _dot(cmb, s_prev.astype(BF16)) * jnp.exp(cs_b[:, :HEAD_DIM])
            cs_last = cs_b[lc - 1:lc, :]
            bd = bm * jnp.exp(cs_last - cs_b)
            s_run[h] = jnp.exp(cs_last[:, :HEAD_DIM]) * s_prev + lax.dot_general(
                bd.astype(BF16), xd, (((0,), (0,)), ((), ())), preferred_element_type=F32)
            y = y + dd[:, hs] * x_h
            o_ref[rs, hs] = y * zs_all[rs, hs]
    for h in range(N_HEADS):
        state[h] = s_run[h]


def _ssd(zs, xbc, dt, dt_bias, a_log, d_skip, bsz, seq):
    n = bsz * seq
    rows = SSD_STEP_CHUNKS * SSD_CHUNK
    nc = seq // rows
    dtb = jnp.zeros((1, LANE), F32).at[0, :N_HEADS].set(dt_bias)
    a = jnp.zeros((1, LANE), F32).at[0, :N_HEADS].set(-jnp.exp(a_log))
    dfull = jnp.repeat(d_skip, HEAD_DIM)[None, :]

    def const(shape):
        return pl.BlockSpec(shape, lambda b, c: (0, 0))

    return pl.pallas_call(
        _ssd_kernel,
        grid=(bsz, nc),
        in_specs=[pl.BlockSpec((rows, GROUP_W), lambda b, c: (b * nc + c, 0)),
                  pl.BlockSpec((rows, SSD_CONV_CH), lambda b, c: (b * nc + c, 0)),
                  pl.BlockSpec((rows, LANE), lambda b, c: (b * nc + c, 0)),
                  const((1, LANE)), const((1, LANE)), const((1, GROUP_W))],
        out_specs=pl.BlockSpec((rows, GROUP_W), lambda b, c: (b * nc + c, 0)),
        out_shape=jax.ShapeDtypeStruct((n, GROUP_W), F32),
        scratch_shapes=[pltpu.VMEM((N_HEADS, SSD_STATE, HEAD_DIM), F32)],
        compiler_params=_cparams(("parallel", "arbitrary")),
        name="ssd",
    )(zs, xbc, dt, dtb, a, dfull)


def _gelu_tanh(x):
    return 0.5 * x * (1.0 + jnp.tanh(math.sqrt(2.0 / math.pi) * (x + 0.044715 * (x * x * x))))


def _group_norm(y, g):
    return (y * lax.rsqrt(jnp.mean(y * y, axis=-1, keepdims=True) + EPS) * g).astype(BF16)


def _merge_rows(h, ys5, s5u, att, yc, ssd, s5d, wglu_ref, mg, wo_ref):
    gw = GROUP_W
    ya = ys5 + s5d * s5u
    ga = _gelu_tanh(ya)
    ya = ga * _sigmoid(_dot(ga.astype(BF16), wglu_ref[...]))
    acc = h
    for j, y in enumerate((ya, att, yc, ssd)):
        acc = acc + _dot(_group_norm(y, mg[:, j * gw:(j + 1) * gw]), wo_ref[j * gw:(j + 1) * gw, :])
    return acc


def _tail_kernel(*refs, tm, seq):
    main = refs[0:6]
    halo = refs[6:12]
    (p_ref, s5d_ref, wglu_ref, mg_ref, wo_ref, g_ref, wg_ref, wu_ref, cw_ref, wd_ref,
     pg_ref, pwg_ref, pwp_ref, o_ref) = refs[12:]
    first = (pl.program_id(0) * tm) % seq == 0
    s5d = s5d_ref[...]
    mg = mg_ref[...]
    rm = tm // ROW_SPLIT
    rows = [slice(a * rm, (a + 1) * rm) for a in range(ROW_SPLIT)]
    hs = [_merge_rows(*[r[rw, :] for r in main], s5d, wglu_ref, mg, wo_ref) for rw in rows]
    hh = _merge_rows(*[r[...] for r in halo], s5d, wglu_ref, mg, wo_ref)
    g = g_ref[...]
    vs = [_rms(h, g).astype(BF16) for h in hs]
    vh = _rms(hh, g).astype(BF16)
    fc = D_FF // FFN_CHUNKS
    ffn = [None] * ROW_SPLIT
    for c in range(FFN_CHUNKS):
        cs = slice(c * fc, (c + 1) * fc)
        gates = [_dot(v, wg_ref[:, cs]) for v in vs]
        gate_h = jnp.where(first, 0.0, _dot(vh, wg_ref[:, cs]))
        ups = [_dot(v, wu_ref[:, cs]) for v in vs]
        acts = []
        for a in range(ROW_SPLIT):
            conv = _causal_conv(gates[a], gate_h, cw_ref, FFN_CONV_K, cs)
            acts.append((_silu(conv) * ups[a]).astype(BF16))
            gate_h = gates[a][rm - HALO:, :]
        for a in range(ROW_SPLIT):
            down = _dot(acts[a], wd_ref[cs, :])
            ffn[a] = down if ffn[a] is None else ffn[a] + down
    for a, rw in enumerate(rows):
        acc = hs[a] + ffn[a]
        gate = _sigmoid(_dot(_rms(acc, pg_ref[...]).astype(BF16), pwg_ref[...]))
        o_ref[rw, :] = acc + gate * _dot(p_ref[rw, :].astype(BF16), pwp_ref[...])


def _tail(h, ys5, s5u, att, yc, ssd, p, layer, s5_d, w_glu, merge_g, w_out, ffn_g, w_gate, w_up, conv_w, w_down,
          ple_g, ple_w_gate, ple_w_proj, seq, tm=512):
    n = h.shape[0]
    per = tm // HALO
    widths = (D_MODEL,) + (GROUP_W,) * 5

    def resident(shape):
        return pl.BlockSpec(shape, lambda i: (0, 0), pipeline_mode=pl.Buffered(1))

    return pl.pallas_call(
        functools.partial(_tail_kernel, tm=tm, seq=seq),
        grid=(n // tm,),
        in_specs=([pl.BlockSpec((tm, wd), lambda i: (i, 0)) for wd in widths]
                  + [pl.BlockSpec((HALO, wd), lambda i: (jnp.maximum(i * per - 1, 0), 0)) for wd in widths]
                  + [pl.BlockSpec((None, tm, PLE_DIM), lambda i: (layer, i, 0)),
                     resident((1, GROUP_W)), resident((GROUP_W, GROUP_W)), resident((1, D_MODEL)),
                     resident((D_MODEL, D_MODEL)), resident((1, D_MODEL)), resident((D_MODEL, D_FF)),
                     resident((D_MODEL, D_FF)), resident((FFN_CONV_K, D_FF)), resident((D_FF, D_MODEL)),
                     resident((1, D_MODEL)), resident((D_MODEL, D_MODEL)), resident((PLE_DIM, D_MODEL))]),
        out_specs=pl.BlockSpec((tm, D_MODEL), lambda i: (i, 0)),
        out_shape=jax.ShapeDtypeStruct((n, D_MODEL), F32),
        compiler_params=_cparams(("parallel",)),
        name="tail",
    )(h, ys5, s5u, att, yc, ssd, h, ys5, s5u, att, yc, ssd, p,
      s5_d[None, :], w_glu, merge_g[None, :], w_out, ffn_g[None, :], w_gate, w_up, conv_w, w_down,
      ple_g[None, :], ple_w_gate, ple_w_proj)


def kernel(x, p, mix_norm_g, w_in, s5_a_re, s5_a_im, s5_log_dt, s5_b_re, s5_b_im, s5_c_re, s5_c_im, s5_d, s5_w_glu, moba_q_g, moba_k_g, sconv_w, ssd_conv_w, ssd_conv_b, ssd_dt_bias, ssd_a_log, ssd_d, merge_norm_g, w_out, ffn_norm_g, ffn_w_gate, ffn_w_up, ffn_conv_w, ffn_w_down, ple_norm_g, ple_w_gate, ple_w_proj):
    bsz, seq, _ = x.shape
    depth = w_in.shape[0]
    n = bsz * seq
    h = x.reshape(n, D_MODEL)
    w_in_b = jnp.pad(w_in, ((0, 0), (0, 0), (0, PROJ_PAD - w_in.shape[2]))).astype(BF16)
    w_glu_b, w_out_b = s5_w_glu.astype(BF16), w_out.astype(BF16)
    w_gate_b, w_up_b, w_down_b = ffn_w_gate.astype(BF16), ffn_w_up.astype(BF16), ffn_w_down.astype(BF16)
    pw_gate_b, pw_proj_b = ple_w_gate.astype(BF16), ple_w_proj.astype(BF16)
    s5_mats = jax.vmap(_s5_param_mats)(s5_a_re, s5_a_im, s5_log_dt, s5_b_re, s5_b_im, s5_c_re, s5_c_im)
    p_rows = p.reshape(depth, n, PLE_DIM)
    for i in range(depth):
        s5u, qkv, yc, zs, xbc, dt = _inproj(h, mix_norm_g[i][None, :], w_in_b[i], sconv_w[i],
                                            ssd_conv_w[i], ssd_conv_b[i], seq)
        ys5 = _s5_scan(s5u, [m[i] for m in s5_mats], bsz, seq)
        att = _moba(qkv, moba_q_g[i][None, :], moba_k_g[i][None, :], bsz, seq)
        yd = _ssd(zs, xbc, dt, ssd_dt_bias[i], ssd_a_log[i], ssd_d[i], bsz, seq)
        h = _tail(h, ys5, s5u, att, yc, yd, p_rows, i, s5_d[i], w_glu_b[i], merge_norm_g[i], w_out_b[i],
                  ffn_norm_g[i], w_gate_b[i], w_up_b[i], ffn_conv_w[i], w_down_b[i],
                  ple_norm_g[i], pw_gate_b[i], pw_proj_b[i], seq)
    return h.reshape(bsz, seq, D_MODEL)
```

```python
import functools
import math

import jax
import jax.numpy as jnp
from jax import lax
from jax.experimental import pallas as pl
from jax.experimental.pallas import tpu as pltpu

F32 = jnp.float32
BF16 = jnp.bfloat16
EPS = 1e-6
NEG = -1e30
LOG2E = 1.4426950408889634
HIGHEST = lax.Precision.HIGHEST

D_MODEL = 1024
PLE_DIM = 256
GROUP_W = 256
N_MIXERS = 4
S5_CH = 16
S5_GROUPS = 16
S5_STATE = 64
S5_L = 8
S5_HG = 8
HEAD_DIM = 64
N_HEADS = 4
MOBA_BLOCK = 256
MOBA_TOPK = 3
SSD_GROUPS = 2
SSD_STATE = 128
SSD_CONV_K = 4
SSD_CHUNK = 128
SSD_STEP_CHUNKS = 2
SSD_CONV_CH = GROUP_W + 2 * SSD_GROUPS * SSD_STATE
SCONV_K = 3
D_FF = 2816
FFN_CONV_K = 3
FFN_SPLITS = (0, 1536, 2816)
HALO = 8
ROW_SPLIT = 1
LANE = 128
PROJ_PAD = 2944
VMEM_LIMIT = 56 * 1024 * 1024


def _cparams(sem):
    return pltpu.CompilerParams(dimension_semantics=sem, vmem_limit_bytes=VMEM_LIMIT)


def _rms(x, g):
    return x * lax.rsqrt(jnp.mean(x * x, axis=-1, keepdims=True) + EPS) * g


def _sigmoid(x):
    return 0.5 * jnp.tanh(0.5 * x) + 0.5


def _silu(x):
    return x * _sigmoid(x)


def _dot(a, b):
    return jnp.dot(a, b, preferred_element_type=F32)


def _dot_nt(a, b, precision=None):
    return lax.dot_general(a, b, (((1,), (1,)), ((), ())), preferred_element_type=F32,
                           precision=precision)


def _shift_rows(x, halo, k):
    if k == 0:
        return x
    rolled = pltpu.roll(x, k, axis=0)
    row = lax.broadcasted_iota(jnp.int32, (HALO, x.shape[1]), 0)
    head = jnp.where(row < k, pltpu.roll(halo, k, axis=0), rolled[:HALO])
    return jnp.concatenate([head, rolled[HALO:]], axis=0)


def _causal_conv(x, halo, w_ref, taps, cols=slice(None)):
    acc = w_ref[taps - 1:taps, cols] * x
    for k in range(taps - 1):
        acc = acc + w_ref[k:k + 1, cols] * _shift_rows(x, halo, taps - 1 - k)
    return acc


def _inproj_kernel(h_ref, hh_ref, g_ref, w_ref, scw_ref, cw_ref, cb_ref,
                   s5u_ref, qkv_ref, yc_ref, zs_ref, xbc_ref, dt_ref, *, tm, seq):
    first = (pl.program_id(0) * tm) % seq == 0
    g = g_ref[...]
    rm = tm // ROW_SPLIT
    rows = [slice(a * rm, (a + 1) * rm) for a in range(ROW_SPLIT)]
    u = [_rms(h_ref[r, :], g).astype(BF16) for r in rows]
    uh = _rms(hh_ref[...], g).astype(BF16)

    def proj(c0, c1):
        return [_dot(ua, w_ref[:, c0:c1]) for ua in u]

    def halo_proj(c0, c1):
        return jnp.where(first, 0.0, _dot(uh, w_ref[:, c0:c1]))

    gw = GROUP_W
    sc = proj(1024, 1792)
    sch = halo_proj(1024, 1792)
    z = proj(1792, 2048)
    cx_halo = sch[:, 2 * gw:] * sch[:, :gw]
    for a, r in enumerate(rows):
        cx = sc[a][:, 2 * gw:] * sc[a][:, :gw]
        yc_ref[r, :] = sc[a][:, gw:2 * gw] * _causal_conv(cx, cx_halo, scw_ref, SCONV_K)
        cx_halo = cx[rm - HALO:, :]
    raw = proj(2048, 2816)
    raw_halo = halo_proj(2048, 2816)
    for a, r in enumerate(rows):
        zs_ref[r, :] = _silu(z[a])
    for ref, (c0, c1) in ((s5u_ref, (0, 256)), (qkv_ref, (256, 1024)), (dt_ref, (2816, PROJ_PAD))):
        for r, y in zip(rows, proj(c0, c1)):
            ref[r, :] = y
    for a, r in enumerate(rows):
        xbc_ref[r, :] = _silu(_causal_conv(raw[a], raw_halo, cw_ref, SSD_CONV_K) + cb_ref[...])
        raw_halo = raw[a][rm - HALO:, :]


def _inproj(h, g, w, layer, sconv_w, ssd_conv_w, ssd_conv_b, seq, tm=512):
    n = h.shape[0]
    per = tm // HALO
    widths = (256, 768, 256, 256, 768, LANE)

    def const(shape):
        return pl.BlockSpec(shape, lambda i: (0, 0))

    return pl.pallas_call(
        functools.partial(_inproj_kernel, tm=tm, seq=seq),
        grid=(n // tm,),
        in_specs=[pl.BlockSpec((tm, D_MODEL), lambda i: (i, 0)),
                  pl.BlockSpec((HALO, D_MODEL), lambda i: (jnp.maximum(i * per - 1, 0), 0)),
                  const((1, D_MODEL)), pl.BlockSpec((None, D_MODEL, PROJ_PAD), lambda i: (layer, 0, 0)),
                  const((SCONV_K, GROUP_W)),
                  const((SSD_CONV_K, SSD_CONV_CH)), const((1, SSD_CONV_CH))],
        out_specs=[pl.BlockSpec((tm, wd), lambda i: (i, 0)) for wd in widths],
        out_shape=[jax.ShapeDtypeStruct((n, wd), F32) for wd in widths],
        compiler_params=_cparams(("parallel",)),
        name="inproj",
    )(h, h, g, w, sconv_w, ssd_conv_w, ssd_conv_b[None, :])


def _s5_param_mats(a_re, a_im, log_dt, b_re, b_im, c_re, c_im):
    L = S5_L
    dt = jnp.exp(log_dt)[:, None]
    taus = jnp.arange(L + 1, dtype=F32)[None, :, None]
    mag = jnp.exp((a_re * dt)[:, None, :] * taus)
    ang = (a_im * dt)[:, None, :] * taus
    pw_r, pw_i = mag * jnp.cos(ang), mag * jnp.sin(ang)
    ab_r, ab_i = pw_r[:, 1], pw_i[:, 1]
    den = a_re * a_re + a_im * a_im
    nr = ab_r - 1.0
    coef_r = (nr * a_re + ab_i * a_im) / den
    coef_i = (ab_i * a_re - nr * a_im) / den
    bb_r = coef_r[..., None] * b_re - coef_i[..., None] * b_im
    bb_i = coef_r[..., None] * b_im + coef_i[..., None] * b_re
    rev_r, rev_i = pw_r[:, L - 1::-1], pw_i[:, L - 1::-1]
    m_r = jnp.einsum("gsp,gpc->gscp", rev_r, bb_r) - jnp.einsum("gsp,gpc->gscp", rev_i, bb_i)
    m_i = jnp.einsum("gsp,gpc->gscp", rev_r, bb_i) + jnp.einsum("gsp,gpc->gscp", rev_i, bb_r)
    cp_r = c_re[:, None] * pw_r[:, :, None, :] - c_im[:, None] * pw_i[:, :, None, :]
    cp_i = c_re[:, None] * pw_i[:, :, None, :] + c_im[:, None] * pw_r[:, :, None, :]
    kk = (jnp.einsum("gtcp,gpd->gtcd", cp_r[:, :L], bb_r, precision=HIGHEST)
          - jnp.einsum("gtcp,gpd->gtcd", cp_i[:, :L], bb_i, precision=HIGHEST))
    s_ix = jnp.arange(L)[:, None]
    l_ix = jnp.arange(L)[None, :]
    lag = l_ix - s_ix
    p_r = cp_r[:, 1:].transpose(0, 3, 1, 2)
    p_i = (-cp_i[:, 1:]).transpose(0, 3, 1, 2)
    hg = S5_HG
    nh = S5_GROUPS // hg
    w = L * hg * S5_CH
    ns = hg * S5_STATE
    g_ix = jnp.arange(hg)[:, None, None]
    place_c = (jnp.arange(LANE)[None, None, :] == g_ix * S5_CH + jnp.arange(S5_CH)[None, :, None]).astype(F32)
    place_p = (jnp.arange(ns)[None, None, :] == g_ix * S5_STATE + jnp.arange(S5_STATE)[None, :, None]).astype(F32)

    def halves(a):
        return a.reshape((nh, hg) + a.shape[1:])

    bd = jnp.einsum("gdi,hgtcd,gcj->htij", place_c, halves(kk), place_c, precision=HIGHEST)
    zero_blk = jnp.zeros_like(bd[:, 0])
    t_mat = jnp.concatenate(
        [jnp.concatenate([bd[:, l - s] if l >= s else zero_blk for l in range(L)], axis=-1) for s in range(L)],
        axis=-2)
    m_mat = jnp.concatenate(
        [jnp.einsum("gdi,hgsdp,gpj->hsij", place_c, halves(m), place_p, precision=HIGHEST).reshape(nh, w, ns)
         for m in (m_r, m_i)], axis=-1)
    p_mat = jnp.concatenate(
        [jnp.concatenate(
            [jnp.einsum("hgpc,gcj->hgpj", halves(q)[:, :, :, l], place_c, precision=HIGHEST).reshape(nh, ns, LANE)
             for l in range(L)], axis=-1) for q in (p_r, p_i)], axis=1)
    al_r = pw_r[:, L].reshape(nh, 1, ns)
    al_i = pw_i[:, L].reshape(nh, 1, ns)
    return t_mat.astype(BF16), m_mat.astype(BF16), p_mat.astype(BF16), al_r, al_i


def _s5_kernel(u_ref, t_ref, m_ref, p_ref, alr_ref, ali_ref, y_ref, loc, sp, sr_s, si_s, *, steps, bsz):
    ns = S5_HG * S5_STATE

    @pl.when(pl.program_id(1) == 0)
    def _():
        sr_s[...] = jnp.zeros_like(sr_s)
        si_s[...] = jnp.zeros_like(si_s)

    nt = ns // LANE
    u = jnp.concatenate([u_ref[:, pl.ds(l, steps, stride=S5_L), :].reshape(bsz * steps, LANE)
                         for l in range(S5_L)], axis=1).astype(BF16)
    loc_v = _dot(u, m_ref[0])
    for j in range(2 * nt):
        loc[j] = loc_v[:, j * LANE:(j + 1) * LANE]
    ar = [jnp.broadcast_to(alr_ref[0, :, j * LANE:(j + 1) * LANE], (bsz, LANE)) for j in range(nt)]
    ai = [jnp.broadcast_to(ali_ref[0, :, j * LANE:(j + 1) * LANE], (bsz, LANE)) for j in range(nt)]

    def step(k, carry):
        sr, si = carry
        rows = pl.ds(k, bsz, stride=steps)
        nr, ni = [], []
        for j in range(nt):
            sp[j, rows, :] = sr[j]
            sp[nt + j, rows, :] = si[j]
            nr.append(ar[j] * sr[j] - ai[j] * si[j] + loc[j, rows, :])
            ni.append(ar[j] * si[j] + ai[j] * sr[j] + loc[nt + j, rows, :])
        return tuple(nr), tuple(ni)

    init = (tuple(sr_s[:, j * LANE:(j + 1) * LANE] for j in range(nt)),
            tuple(si_s[:, j * LANE:(j + 1) * LANE] for j in range(nt)))
    sr, si = lax.fori_loop(0, steps, step, init)
    for j in range(nt):
        sr_s[:, j * LANE:(j + 1) * LANE] = sr[j]
        si_s[:, j * LANE:(j + 1) * LANE] = si[j]
    s_prev = jnp.concatenate([sp[j] for j in range(2 * nt)], axis=1).astype(BF16)
    y = _dot(u, t_ref[0]) + _dot(s_prev, p_ref[0])
    for l in range(S5_L):
        y_ref[:, pl.ds(l, steps, stride=S5_L), :] = y[:, l * LANE:(l + 1) * LANE].reshape(bsz, steps, LANE)


def _s5_scan(s5u, mats, layer, bsz, seq, steps=64):
    n_chunks = seq // S5_L
    w = S5_L * LANE
    ns = S5_HG * S5_STATE
    nh = GROUP_W // LANE
    t_mat, m_mat, p_mat, al_r, al_i = mats

    def hspec(a, b):
        return pl.BlockSpec((None, 1, a, b), lambda hf, j: (layer, hf, 0, 0))

    rspec = pl.BlockSpec((bsz, steps * S5_L, LANE), lambda hf, j: (0, j, hf))
    y = pl.pallas_call(
        functools.partial(_s5_kernel, steps=steps, bsz=bsz),
        grid=(nh, n_chunks // steps),
        in_specs=[rspec, hspec(w, w), hspec(w, 2 * ns), hspec(2 * ns, w), hspec(1, ns), hspec(1, ns)],
        out_specs=rspec,
        out_shape=jax.ShapeDtypeStruct((bsz, seq, GROUP_W), F32),
        scratch_shapes=[pltpu.VMEM((2 * ns // LANE, bsz * steps, LANE), F32),
                        pltpu.VMEM((2 * ns // LANE, bsz * steps, LANE), F32),
                        pltpu.VMEM((bsz, ns), F32), pltpu.VMEM((bsz, ns), F32)],
        compiler_params=_cparams(("parallel", "arbitrary")),
        name="s5_scan",
    )(s5u.reshape(bsz, seq, GROUP_W), t_mat, m_mat, p_mat, al_r, al_i)
    return y.reshape(bsz * seq, GROUP_W)


def _moba_kernel(q_ref, k_ref, v_ref, qg_ref, kg_ref, o_ref, kx, vt, kmean, s_a, s_b, acc_s, m_s, l_s, *, n_blocks):
    i = pl.program_id(1)
    blk = MOBA_BLOCK
    hd = HEAD_DIM
    seq = n_blocks * blk

    @pl.when(i == 0)
    def _():
        kg = kg_ref[...]
        lane = lax.broadcasted_iota(jnp.int32, (seq, hd), 1)
        kblk = lax.broadcasted_iota(jnp.int32, (seq, hd), 0) // blk
        onehot = jnp.where(lane == kblk, 1.0, 0.0).astype(BF16)
        for h in range(N_HEADS):
            kh = _rms(k_ref[:, h * hd:(h + 1) * hd], kg)
            kx[h, :, 0:hd] = kh.astype(BF16)
            kx[h, :, hd:2 * hd] = onehot
            kmean[h] = jnp.mean(kh.reshape(n_blocks, blk, hd), axis=1)
        for nb in range(n_blocks):
            vt[:, nb * blk:(nb + 1) * blk] = v_ref[nb * blk:(nb + 1) * blk, :].T.astype(BF16)

    q_t = q_ref[...].T
    qg = qg_ref[...]
    key_ix = lax.broadcasted_iota(jnp.int32, (blk, blk), 0)
    qry_ix = lax.broadcasted_iota(jnp.int32, (blk, blk), 1)
    causal = key_ix <= qry_ix
    bid = lax.broadcasted_iota(jnp.int32, (n_blocks, blk), 0)
    past = bid < i
    own0 = pl.multiple_of(i * blk, blk)

    qns, qss, s_own = [], [], []
    for h in range(N_HEADS):
        qh = q_t[h * hd:(h + 1) * hd, :]
        qn = qh * lax.rsqrt(jnp.mean(qh * qh, axis=0, keepdims=True) + EPS) * qg
        qs = qn * (hd ** -0.5 * LOG2E)
        qns.append(qn)
        qss.append(qs)
        s_own.append(_dot(kx[h, pl.ds(own0, blk), :],
                          jnp.concatenate([qs, jnp.zeros((hd, blk), F32)], axis=0).astype(BF16)))

    qxs = []
    for h in range(N_HEADS):
        qn = qns[h]
        gate = jnp.dot(kmean[h], qn, preferred_element_type=F32, precision=HIGHEST)
        gate = jnp.where(past, gate, -jnp.inf)
        cnt = jnp.zeros((n_blocks, blk), F32)
        for m in range(n_blocks):
            gm = gate[m:m + 1, :]
            cnt = cnt + jnp.where(gm > gate, 1.0, jnp.where((gm == gate) & (bid > m), 1.0, 0.0))
        bias = jnp.where((bid == i) | (past & (cnt < MOBA_TOPK)), 0.0, NEG)
        qxs.append(jnp.concatenate([qss[h], bias, jnp.zeros((hd - n_blocks, blk), F32)],
                                   axis=0).astype(BF16))

    heads = range(N_HEADS)

    def issue_scores(n, dst):
        k0 = pl.multiple_of(n * blk, blk)
        for h in heads:
            dst[h] = _dot(kx[h, pl.ds(k0, blk), :], qxs[h])

    def values(n, ps):
        k0 = pl.multiple_of(n * blk, blk)
        return [_dot(vt[h * hd:(h + 1) * hd, pl.ds(k0, blk)], ps[h]) for h in heads]

    ps = []
    for h in heads:
        s = jnp.where(causal, s_own[h], NEG)
        m0 = jnp.max(s, axis=0, keepdims=True)
        p = jnp.exp2(s - m0)
        m_s[h] = m0
        l_s[h] = jnp.sum(p, axis=0, keepdims=True)
        ps.append(p.astype(BF16))
    for h, pv in zip(heads, values(i, ps)):
        acc_s[h] = pv

    def step(n, src, dst):
        issue_scores(jnp.minimum(n + 1, jnp.maximum(i - 1, 0)), dst)
        ps, alphas = [], []
        for h in heads:
            m_run = m_s[h]
            s = src[h]
            m_new = jnp.maximum(m_run, jnp.max(s, axis=0, keepdims=True))
            alpha = jnp.exp2(m_run - m_new)
            p = jnp.exp2(s - m_new)
            m_s[h] = m_new
            l_s[h] = alpha * l_s[h] + jnp.sum(p, axis=0, keepdims=True)
            alphas.append(alpha)
            ps.append(p.astype(BF16))
        for h, pv in zip(heads, values(n, ps)):
            acc_s[h] = alphas[h] * acc_s[h] + pv

    issue_scores(0, s_a)

    def pair(k, carry):
        step(2 * k, s_a, s_b)
        step(2 * k + 1, s_b, s_a)
        return carry

    lax.fori_loop(0, i // 2, pair, 0)

    @pl.when(i % 2 == 1)
    def _():
        step(i - 1, s_a, s_b)

    o_ref[...] = jnp.concatenate([acc_s[h] / l_s[h] for h in heads], axis=0).T


def _moba(qkv, q_g, k_g, bsz, seq):
    n_blocks = seq // MOBA_BLOCK
    n = bsz * seq
    return pl.pallas_call(
        functools.partial(_moba_kernel, n_blocks=n_blocks),
        grid=(bsz, n_blocks),
        in_specs=[pl.BlockSpec((MOBA_BLOCK, GROUP_W), lambda b, i: (b * n_blocks + i, 0)),
                  pl.BlockSpec((seq, GROUP_W), lambda b, i: (b, 1)),
                  pl.BlockSpec((seq, GROUP_W), lambda b, i: (b, 2)),
                  pl.BlockSpec((HEAD_DIM, 1), lambda b, i: (0, 0)),
                  pl.BlockSpec((1, HEAD_DIM), lambda b, i: (0, 0))],
        out_specs=pl.BlockSpec((MOBA_BLOCK, GROUP_W), lambda b, i: (b * n_blocks + i, 0)),
        out_shape=jax.ShapeDtypeStruct((n, GROUP_W), F32),
        scratch_shapes=[pltpu.VMEM((N_HEADS, seq, 2 * HEAD_DIM), BF16),
                        pltpu.VMEM((GROUP_W, seq), BF16),
                        pltpu.VMEM((N_HEADS, n_blocks, HEAD_DIM), F32),
                        pltpu.VMEM((N_HEADS, MOBA_BLOCK, MOBA_BLOCK), F32),
                        pltpu.VMEM((N_HEADS, MOBA_BLOCK, MOBA_BLOCK), F32),
                        pltpu.VMEM((N_HEADS, HEAD_DIM, MOBA_BLOCK), F32),
                        pltpu.VMEM((N_HEADS, 1, MOBA_BLOCK), F32),
                        pltpu.VMEM((N_HEADS, 1, MOBA_BLOCK), F32)],
        compiler_params=_cparams(("parallel", "arbitrary")),
        name="moba",
    )(qkv, qkv, qkv, q_g.reshape(HEAD_DIM, 1), k_g.reshape(1, HEAD_DIM))


def _ssd_kernel(zs_ref, xbc_ref, dt_ref, dtb_ref, a_ref, d_ref, o_ref, state):
    c = pl.program_id(1)
    lc = SSD_CHUNK
    gn = SSD_STATE
    hd = HEAD_DIM
    hpg = N_HEADS // SSD_GROUPS
    chunks = range(SSD_STEP_CHUNKS)
    heads = range(N_HEADS)

    @pl.when(c == 0)
    def _():
        state[...] = jnp.zeros_like(state)

    xbc_all = xbc_ref[...]
    dtr = dt_ref[...] + dtb_ref[...]
    dt_all = jnp.maximum(dtr, 0.0) + jnp.log(1.0 + jnp.exp(-jnp.abs(dtr)))
    adt_all = dt_all * a_ref[...]
    row = lax.broadcasted_iota(jnp.int32, (lc, lc), 0)
    col = lax.broadcasted_iota(jnp.int32, (lc, lc), 1)
    causal = col <= row
    tril = jnp.where(causal, 1.0, 0.0)
    zs_all = zs_ref[...]
    dd = d_ref[...]
    rows = [slice(cc * lc, (cc + 1) * lc) for cc in chunks]
    xbc = [xbc_all[r] for r in rows]
    bmat = [[x[:, GROUP_W + g * gn:GROUP_W + (g + 1) * gn] for g in range(SSD_GROUPS)] for x in xbc]
    cmat = [[x[:, GROUP_W + (SSD_GROUPS + g) * gn:GROUP_W + (SSD_GROUPS + g + 1) * gn].astype(BF16)
             for g in range(SSD_GROUPS)] for x in xbc]
    s_in = [state[h] for h in heads]
    y_off = [[_dot(cmat[0][h // hpg], s_in[h].astype(BF16)) for h in heads]]
    cb = [[_dot_nt(cmat[cc][g], bmat[cc][g].astype(BF16)) for g in range(SSD_GROUPS)] for cc in chunks]
    cs = [jnp.dot(tril, adt_all[r], preferred_element_type=F32, precision=HIGHEST) for r in rows]
    cs_t = [x.T for x in cs]
    y_diag, upd, cs_b, x_h = [], [], [], []
    for cc in chunks:
        yd_c, upd_c, csb_c, xh_c = [], [], [], []
        for h in heads:
            g = h // hpg
            csb = jnp.broadcast_to(cs[cc][:, h:h + 1], (lc, lc))
            seg = csb - cs_t[cc][h:h + 1, :]
            decay = jnp.where(causal, jnp.exp(jnp.where(causal, seg, 0.0)), 0.0)
            scores = (cb[cc][g] * decay).astype(BF16)
            xh = xbc[cc][:, h * hd:(h + 1) * hd]
            xd = (xh * jnp.broadcast_to(dt_all[rows[cc]][:, h:h + 1], (lc, hd))).astype(BF16)
            bd = bmat[cc][g] * jnp.exp(csb[lc - 1:lc, :] - csb)
            yd_c.append(_dot(scores, xd))
            upd_c.append(lax.dot_general(bd.astype(BF16), xd, (((0,), (0,)), ((), ())), preferred_element_type=F32))
            csb_c.append(csb)
            xh_c.append(xh)
        y_diag.append(yd_c); upd.append(upd_c); cs_b.append(csb_c); x_h.append(xh_c)
    s_run = s_in
    for cc in chunks:
        if cc > 0:
            y_off.append([_dot(cmat[cc][h // hpg], s_run[h].astype(BF16)) for h in heads])
        s_run = [jnp.exp(cs_b[cc][h][lc - 1:lc, :hd]) * s_run[h] + upd[cc][h] for h in heads]
    for h in heads:
        state[h] = s_run[h]
    for cc in chunks:
        for h in heads:
            hs = slice(h * hd, (h + 1) * hd)
            y = y_diag[cc][h] + y_off[cc][h] * jnp.exp(cs_b[cc][h][:, :hd]) + dd[:, hs] * x_h[cc][h]
            o_ref[rows[cc], hs] = y * zs_all[rows[cc], hs]


def _ssd(zs, xbc, dt, dt_bias, a_log, d_skip, bsz, seq):
    n = bsz * seq
    rows = SSD_STEP_CHUNKS * SSD_CHUNK
    nc = seq // rows
    dtb = jnp.zeros((1, LANE), F32).at[0, :N_HEADS].set(dt_bias)
    a = jnp.zeros((1, LANE), F32).at[0, :N_HEADS].set(-jnp.exp(a_log))
    dfull = jnp.repeat(d_skip, HEAD_DIM)[None, :]

    def const(shape):
        return pl.BlockSpec(shape, lambda b, c: (0, 0))

    return pl.pallas_call(
        _ssd_kernel,
        grid=(bsz, nc),
        in_specs=[pl.BlockSpec((rows, GROUP_W), lambda b, c: (b * nc + c, 0)),
                  pl.BlockSpec((rows, SSD_CONV_CH), lambda b, c: (b * nc + c, 0)),
                  pl.BlockSpec((rows, LANE), lambda b, c: (b * nc + c, 0)),
                  const((1, LANE)), const((1, LANE)), const((1, GROUP_W))],
        out_specs=pl.BlockSpec((rows, GROUP_W), lambda b, c: (b * nc + c, 0)),
        out_shape=jax.ShapeDtypeStruct((n, GROUP_W), F32),
        scratch_shapes=[pltpu.VMEM((N_HEADS, SSD_STATE, HEAD_DIM), F32)],
        compiler_params=_cparams(("parallel", "arbitrary")),
        name="ssd",
    )(zs, xbc, dt, dtb, a, dfull)


def _gelu_tanh(x):
    return 0.5 * x * (1.0 + jnp.tanh(math.sqrt(2.0 / math.pi) * (x + 0.044715 * (x * x * x))))


def _group_norm(y, g):
    return (y * lax.rsqrt(jnp.mean(y * y, axis=-1, keepdims=True) + EPS) * g).astype(BF16)


def _merge_rows(h, ys5, s5u, att, yc, ssd, s5d, wglu_ref, mg, wo_ref):
    gw = GROUP_W
    ya = ys5 + s5d * s5u
    ga = _gelu_tanh(ya)
    ya = ga * _sigmoid(_dot(ga.astype(BF16), wglu_ref[...]))
    acc = h
    for j, y in enumerate((ya, att, yc, ssd)):
        acc = acc + _dot(_group_norm(y, mg[:, j * gw:(j + 1) * gw]), wo_ref[j * gw:(j + 1) * gw, :])
    return acc


def _tail_kernel(*refs, tm, seq):
    main = refs[0:6]
    halo = refs[6:12]
    (p_ref, s5d_ref, wglu_ref, mg_ref, wo_ref, g_ref, wg_ref, wu_ref, cw_ref, wd_ref,
     pg_ref, pwg_ref, pwp_ref, o_ref) = refs[12:]
    first = (pl.program_id(0) * tm) % seq == 0
    s5d = s5d_ref[...]
    mg = mg_ref[...]
    rm = tm // ROW_SPLIT
    rows = [slice(a * rm, (a + 1) * rm) for a in range(ROW_SPLIT)]
    hs = [_merge_rows(*[r[rw, :] for r in main], s5d, wglu_ref, mg, wo_ref) for rw in rows]
    hh = _merge_rows(*[r[...] for r in halo], s5d, wglu_ref, mg, wo_ref)
    g = g_ref[...]
    vs = [_rms(h, g).astype(BF16) for h in hs]
    vh = _rms(hh, g).astype(BF16)
    ffn = [None] * ROW_SPLIT
    splits = [slice(c0, c1) for c0, c1 in zip(FFN_SPLITS[:-1], FFN_SPLITS[1:])]

    def gate_proj(cs):
        gates = [_dot(v, wg_ref[:, cs]) for v in vs]
        return gates, jnp.where(first, 0.0, _dot(vh, wg_ref[:, cs]))

    nxt = gate_proj(splits[0])
    for ci, cs in enumerate(splits):
        gates, gate_h = nxt
        ups = [_dot(v, wu_ref[:, cs]) for v in vs]
        if ci + 1 < len(splits):
            nxt = gate_proj(splits[ci + 1])
        acts = []
        for a in range(ROW_SPLIT):
            conv = _causal_conv(gates[a], gate_h, cw_ref, FFN_CONV_K, cs)
            acts.append((_silu(conv) * ups[a]).astype(BF16))
            gate_h = gates[a][rm - HALO:, :]
        for a in range(ROW_SPLIT):
            down = _dot(acts[a], wd_ref[cs, :])
            ffn[a] = down if ffn[a] is None else ffn[a] + down
    for a, rw in enumerate(rows):
        acc = hs[a] + ffn[a]
        gate = _sigmoid(_dot(_rms(acc, pg_ref[...]).astype(BF16), pwg_ref[...]))
        o_ref[rw, :] = acc + gate * _dot(p_ref[rw, :].astype(BF16), pwp_ref[...])


def _tail(h, ys5, s5u, att, yc, ssd, p, layer, s5_d, w_glu, merge_g, w_out, ffn_g, w_gate, w_up, conv_w, w_down,
          ple_g, ple_w_gate, ple_w_proj, seq, tm=512):
    n = h.shape[0]
    per = tm // HALO
    widths = (D_MODEL,) + (GROUP_W,) * 5

    def resident(shape):
        return pl.BlockSpec(shape, lambda i: (0, 0), pipeline_mode=pl.Buffered(1))

    def stacked(shape):
        return pl.BlockSpec((None,) + shape, lambda i: (layer, 0, 0), pipeline_mode=pl.Buffered(1))

    return pl.pallas_call(
        functools.partial(_tail_kernel, tm=tm, seq=seq),
        grid=(n // tm,),
        in_specs=([pl.BlockSpec((tm, wd), lambda i: (i, 0)) for wd in widths]
                  + [pl.BlockSpec((HALO, wd), lambda i: (jnp.maximum(i * per - 1, 0), 0)) for wd in widths]
                  + [pl.BlockSpec((None, tm, PLE_DIM), lambda i: (layer, i, 0)),
                     resident((1, GROUP_W)), stacked((GROUP_W, GROUP_W)), resident((1, D_MODEL)),
                     stacked((D_MODEL, D_MODEL)), resident((1, D_MODEL)), stacked((D_MODEL, D_FF)),
                     stacked((D_MODEL, D_FF)), resident((FFN_CONV_K, D_FF)), stacked((D_FF, D_MODEL)),
                     resident((1, D_MODEL)), stacked((D_MODEL, D_MODEL)), stacked((PLE_DIM, D_MODEL))]),
        out_specs=pl.BlockSpec((tm, D_MODEL), lambda i: (i, 0)),
        out_shape=jax.ShapeDtypeStruct((n, D_MODEL), F32),
        compiler_params=_cparams(("parallel",)),
        name="tail",
    )(h, ys5, s5u, att, yc, ssd, h, ys5, s5u, att, yc, ssd, p,
      s5_d[None, :], w_glu, merge_g[None, :], w_out, ffn_g[None, :], w_gate, w_up, conv_w, w_down,
      ple_g[None, :], ple_w_gate, ple_w_proj)


def kernel(x, p, mix_norm_g, w_in, s5_a_re, s5_a_im, s5_log_dt, s5_b_re, s5_b_im, s5_c_re, s5_c_im, s5_d, s5_w_glu, moba_q_g, moba_k_g, sconv_w, ssd_conv_w, ssd_conv_b, ssd_dt_bias, ssd_a_log, ssd_d, merge_norm_g, w_out, ffn_norm_g, ffn_w_gate, ffn_w_up, ffn_conv_w, ffn_w_down, ple_norm_g, ple_w_gate, ple_w_proj):
    bsz, seq, _ = x.shape
    depth = w_in.shape[0]
    n = bsz * seq
    h = x.reshape(n, D_MODEL)
    w_in_b = jnp.pad(w_in, ((0, 0), (0, 0), (0, PROJ_PAD - w_in.shape[2]))).astype(BF16)
    w_glu_b, w_out_b = s5_w_glu.astype(BF16), w_out.astype(BF16)
    w_gate_b, w_up_b, w_down_b = ffn_w_gate.astype(BF16), ffn_w_up.astype(BF16), ffn_w_down.astype(BF16)
    pw_gate_b, pw_proj_b = ple_w_gate.astype(BF16), ple_w_proj.astype(BF16)
    s5_mats = jax.vmap(_s5_param_mats)(s5_a_re, s5_a_im, s5_log_dt, s5_b_re, s5_b_im, s5_c_re, s5_c_im)
    p_rows = p.reshape(depth, n, PLE_DIM)
    for i in range(depth):
        s5u, qkv, yc, zs, xbc, dt = _inproj(h, mix_norm_g[i][None, :], w_in_b, i, sconv_w[i],
                                            ssd_conv_w[i], ssd_conv_b[i], seq)
        ys5 = _s5_scan(s5u, s5_mats, i, bsz, seq)
        att = _moba(qkv, moba_q_g[i][None, :], moba_k_g[i][None, :], bsz, seq)
        yd = _ssd(zs, xbc, dt, ssd_dt_bias[i], ssd_a_log[i], ssd_d[i], bsz, seq)
        h = _tail(h, ys5, s5u, att, yc, yd, p_rows, i, s5_d[i], w_glu_b, merge_norm_g[i], w_out_b,
                  ffn_norm_g[i], w_gate_b, w_up_b, ffn_conv_w[i], w_down_b,
                  ple_norm_g[i], pw_gate_b, pw_proj_b, seq)
    return h.reshape(bsz, seq, D_MODEL)
```

```python
import functools
import math

import jax
import jax.numpy as jnp
from jax import lax
from jax.experimental import pallas as pl
from jax.experimental.pallas import tpu as pltpu

F32 = jnp.float32
BF16 = jnp.bfloat16
EPS = 1e-6
NEG = -1e30
LOG2E = 1.4426950408889634
HIGHEST = lax.Precision.HIGHEST

D_MODEL = 1024
PLE_DIM = 256
GROUP_W = 256
N_MIXERS = 4
S5_CH = 16
S5_GROUPS = 16
S5_STATE = 64
S5_L = 8
S5_HG = 8
HEAD_DIM = 64
N_HEADS = 4
MOBA_BLOCK = 256
MOBA_TOPK = 3
MOBA_VROWS = 80
SSD_GROUPS = 2
SSD_STATE = 128
SSD_CONV_K = 4
SSD_CHUNK = 128
SSD_STEP_CHUNKS = 2
SSD_CONV_CH = GROUP_W + 2 * SSD_GROUPS * SSD_STATE
SCONV_K = 3
D_FF = 2816
FFN_CONV_K = 3
FFN_SPLITS = (0, 1536, 2816)
HALO = 8
ROW_SPLIT = 2
LANE = 128
PROJ_PAD = 2944
VMEM_LIMIT = 56 * 1024 * 1024


def _cparams(sem):
    return pltpu.CompilerParams(dimension_semantics=sem, vmem_limit_bytes=VMEM_LIMIT)


def _rms(x, g):
    return x * lax.rsqrt(jnp.mean(x * x, axis=-1, keepdims=True) + EPS) * g


def _sigmoid(x):
    return 0.5 * jnp.tanh(0.5 * x) + 0.5


def _silu(x):
    return x * _sigmoid(x)


def _dot(a, b):
    return jnp.dot(a, b, preferred_element_type=F32)


def _dot_nt(a, b, precision=None):
    return lax.dot_general(a, b, (((1,), (1,)), ((), ())), preferred_element_type=F32,
                           precision=precision)


def _shift_rows(x, halo, k):
    if k == 0:
        return x
    rolled = pltpu.roll(x, k, axis=0)
    row = lax.broadcasted_iota(jnp.int32, (HALO, x.shape[1]), 0)
    head = jnp.where(row < k, pltpu.roll(halo, k, axis=0), rolled[:HALO])
    return jnp.concatenate([head, rolled[HALO:]], axis=0)


def _causal_conv(x, halo, w_ref, taps, cols=slice(None)):
    acc = w_ref[taps - 1:taps, cols] * x
    for k in range(taps - 1):
        acc = acc + w_ref[k:k + 1, cols] * _shift_rows(x, halo, taps - 1 - k)
    return acc


def _inproj_kernel(h_ref, hh_ref, g_ref, w_ref, scw_ref, cw_ref, cb_ref,
                   s5u_ref, qkv_ref, yc_ref, zs_ref, xbc_ref, dt_ref, *, tm, seq):
    first = (pl.program_id(0) * tm) % seq == 0
    g = g_ref[...]
    rm = tm // ROW_SPLIT
    rows = [slice(a * rm, (a + 1) * rm) for a in range(ROW_SPLIT)]
    u = [_rms(h_ref[r, :], g).astype(BF16) for r in rows]
    uh = _rms(hh_ref[...], g).astype(BF16)

    def proj(c0, c1):
        return [_dot(ua, w_ref[:, c0:c1]) for ua in u]

    def halo_proj(c0, c1):
        return jnp.where(first, 0.0, _dot(uh, w_ref[:, c0:c1]))

    gw = GROUP_W
    sc = proj(1024, 1792)
    sch = halo_proj(1024, 1792)
    z = proj(1792, 2048)
    cx_halo = sch[:, 2 * gw:] * sch[:, :gw]
    for a, r in enumerate(rows):
        cx = sc[a][:, 2 * gw:] * sc[a][:, :gw]
        yc_ref[r, :] = sc[a][:, gw:2 * gw] * _causal_conv(cx, cx_halo, scw_ref, SCONV_K)
        cx_halo = cx[rm - HALO:, :]
    raw = proj(2048, 2816)
    raw_halo = halo_proj(2048, 2816)
    for a, r in enumerate(rows):
        zs_ref[r, :] = _silu(z[a])
    for ref, (c0, c1) in ((s5u_ref, (0, 256)), (qkv_ref, (256, 1024)), (dt_ref, (2816, PROJ_PAD))):
        for r, y in zip(rows, proj(c0, c1)):
            ref[r, :] = y
    for a, r in enumerate(rows):
        xbc_ref[r, :] = _silu(_causal_conv(raw[a], raw_halo, cw_ref, SSD_CONV_K) + cb_ref[...])
        raw_halo = raw[a][rm - HALO:, :]


def _inproj(h, g, w, layer, sconv_w, ssd_conv_w, ssd_conv_b, seq, tm=512):
    n = h.shape[0]
    per = tm // HALO
    widths = (256, 768, 256, 256, 768, LANE)

    def const(shape):
        return pl.BlockSpec(shape, lambda i: (0, 0))

    return pl.pallas_call(
        functools.partial(_inproj_kernel, tm=tm, seq=seq),
        grid=(n // tm,),
        in_specs=[pl.BlockSpec((tm, D_MODEL), lambda i: (i, 0)),
                  pl.BlockSpec((HALO, D_MODEL), lambda i: (jnp.maximum(i * per - 1, 0), 0)),
                  const((1, D_MODEL)), pl.BlockSpec((None, D_MODEL, PROJ_PAD), lambda i: (layer, 0, 0)),
                  const((SCONV_K, GROUP_W)),
                  const((SSD_CONV_K, SSD_CONV_CH)), const((1, SSD_CONV_CH))],
        out_specs=[pl.BlockSpec((tm, wd), lambda i: (i, 0)) for wd in widths],
        out_shape=[jax.ShapeDtypeStruct((n, wd), F32) for wd in widths],
        compiler_params=_cparams(("parallel",)),
        name="inproj",
    )(h, h, g, w, sconv_w, ssd_conv_w, ssd_conv_b[None, :])


def _s5_param_mats(a_re, a_im, log_dt, b_re, b_im, c_re, c_im):
    L = S5_L
    dt = jnp.exp(log_dt)[:, None]
    taus = jnp.arange(L + 1, dtype=F32)[None, :, None]
    mag = jnp.exp((a_re * dt)[:, None, :] * taus)
    ang = (a_im * dt)[:, None, :] * taus
    pw_r, pw_i = mag * jnp.cos(ang), mag * jnp.sin(ang)
    ab_r, ab_i = pw_r[:, 1], pw_i[:, 1]
    den = a_re * a_re + a_im * a_im
    nr = ab_r - 1.0
    coef_r = (nr * a_re + ab_i * a_im) / den
    coef_i = (ab_i * a_re - nr * a_im) / den
    bb_r = coef_r[..., None] * b_re - coef_i[..., None] * b_im
    bb_i = coef_r[..., None] * b_im + coef_i[..., None] * b_re
    rev_r, rev_i = pw_r[:, L - 1::-1], pw_i[:, L - 1::-1]
    m_r = jnp.einsum("gsp,gpc->gscp", rev_r, bb_r) - jnp.einsum("gsp,gpc->gscp", rev_i, bb_i)
    m_i = jnp.einsum("gsp,gpc->gscp", rev_r, bb_i) + jnp.einsum("gsp,gpc->gscp", rev_i, bb_r)
    cp_r = c_re[:, None] * pw_r[:, :, None, :] - c_im[:, None] * pw_i[:, :, None, :]
    cp_i = c_re[:, None] * pw_i[:, :, None, :] + c_im[:, None] * pw_r[:, :, None, :]
    kk = (jnp.einsum("gtcp,gpd->gtcd", cp_r[:, :L], bb_r, precision=HIGHEST)
          - jnp.einsum("gtcp,gpd->gtcd", cp_i[:, :L], bb_i, precision=HIGHEST))
    s_ix = jnp.arange(L)[:, None]
    l_ix = jnp.arange(L)[None, :]
    lag = l_ix - s_ix
    p_r = cp_r[:, 1:].transpose(0, 3, 1, 2)
    p_i = (-cp_i[:, 1:]).transpose(0, 3, 1, 2)
    hg = S5_HG
    nh = S5_GROUPS // hg
    w = L * hg * S5_CH
    ns = hg * S5_STATE
    g_ix = jnp.arange(hg)[:, None, None]
    place_c = (jnp.arange(LANE)[None, None, :] == g_ix * S5_CH + jnp.arange(S5_CH)[None, :, None]).astype(F32)
    place_p = (jnp.arange(ns)[None, None, :] == g_ix * S5_STATE + jnp.arange(S5_STATE)[None, :, None]).astype(F32)

    def halves(a):
        return a.reshape((nh, hg) + a.shape[1:])

    bd = jnp.einsum("gdi,hgtcd,gcj->htij", place_c, halves(kk), place_c, precision=HIGHEST)
    zero_blk = jnp.zeros_like(bd[:, 0])
    t_mat = jnp.concatenate(
        [jnp.concatenate([bd[:, l - s] if l >= s else zero_blk for l in range(L)], axis=-1) for s in range(L)],
        axis=-2)
    m_mat = jnp.concatenate(
        [jnp.einsum("gdi,hgsdp,gpj->hsij", place_c, halves(m), place_p, precision=HIGHEST).reshape(nh, w, ns)
         for m in (m_r, m_i)], axis=-1)
    p_mat = jnp.concatenate(
        [jnp.concatenate(
            [jnp.einsum("hgpc,gcj->hgpj", halves(q)[:, :, :, l], place_c, precision=HIGHEST).reshape(nh, ns, LANE)
             for l in range(L)], axis=-1) for q in (p_r, p_i)], axis=1)
    al_r = pw_r[:, L].reshape(nh, 1, ns)
    al_i = pw_i[:, L].reshape(nh, 1, ns)
    return t_mat.astype(BF16), m_mat.astype(BF16), p_mat.astype(BF16), al_r, al_i


def _s5_kernel(u_ref, t_ref, m_ref, p_ref, alr_ref, ali_ref, y_ref, loc, sp, sr_s, si_s, *, steps, bsz):
    ns = S5_HG * S5_STATE

    @pl.when(pl.program_id(1) == 0)
    def _():
        sr_s[...] = jnp.zeros_like(sr_s)
        si_s[...] = jnp.zeros_like(si_s)

    nt = ns // LANE
    u = jnp.concatenate([u_ref[:, pl.ds(l, steps, stride=S5_L), :].reshape(bsz * steps, LANE)
                         for l in range(S5_L)], axis=1).astype(BF16)
    loc_v = _dot(u, m_ref[0])
    for j in range(2 * nt):
        loc[j] = loc_v[:, j * LANE:(j + 1) * LANE]
    ar = [jnp.broadcast_to(alr_ref[0, :, j * LANE:(j + 1) * LANE], (bsz, LANE)) for j in range(nt)]
    ai = [jnp.broadcast_to(ali_ref[0, :, j * LANE:(j + 1) * LANE], (bsz, LANE)) for j in range(nt)]

    def step(k, carry):
        sr, si = carry
        rows = pl.ds(k, bsz, stride=steps)
        nr, ni = [], []
        for j in range(nt):
            sp[j, rows, :] = sr[j]
            sp[nt + j, rows, :] = si[j]
            nr.append(ar[j] * sr[j] - ai[j] * si[j] + loc[j, rows, :])
            ni.append(ar[j] * si[j] + ai[j] * sr[j] + loc[nt + j, rows, :])
        return tuple(nr), tuple(ni)

    init = (tuple(sr_s[:, j * LANE:(j + 1) * LANE] for j in range(nt)),
            tuple(si_s[:, j * LANE:(j + 1) * LANE] for j in range(nt)))
    sr, si = lax.fori_loop(0, steps, step, init, unroll=4)
    for j in range(nt):
        sr_s[:, j * LANE:(j + 1) * LANE] = sr[j]
        si_s[:, j * LANE:(j + 1) * LANE] = si[j]
    s_prev = jnp.concatenate([sp[j] for j in range(2 * nt)], axis=1).astype(BF16)
    y = _dot(u, t_ref[0]) + _dot(s_prev, p_ref[0])
    for l in range(S5_L):
        y_ref[:, pl.ds(l, steps, stride=S5_L), :] = y[:, l * LANE:(l + 1) * LANE].reshape(bsz, steps, LANE)


def _s5_scan(s5u, mats, layer, bsz, seq, steps=64):
    n_chunks = seq // S5_L
    w = S5_L * LANE
    ns = S5_HG * S5_STATE
    nh = GROUP_W // LANE
    t_mat, m_mat, p_mat, al_r, al_i = mats

    def hspec(a, b):
        return pl.BlockSpec((None, 1, a, b), lambda hf, j: (layer, hf, 0, 0))

    rspec = pl.BlockSpec((bsz, steps * S5_L, LANE), lambda hf, j: (0, j, hf))
    y = pl.pallas_call(
        functools.partial(_s5_kernel, steps=steps, bsz=bsz),
        grid=(nh, n_chunks // steps),
        in_specs=[rspec, hspec(w, w), hspec(w, 2 * ns), hspec(2 * ns, w), hspec(1, ns), hspec(1, ns)],
        out_specs=rspec,
        out_shape=jax.ShapeDtypeStruct((bsz, seq, GROUP_W), F32),
        scratch_shapes=[pltpu.VMEM((2 * ns // LANE, bsz * steps, LANE), F32),
                        pltpu.VMEM((2 * ns // LANE, bsz * steps, LANE), F32),
                        pltpu.VMEM((bsz, ns), F32), pltpu.VMEM((bsz, ns), F32)],
        compiler_params=_cparams(("parallel", "arbitrary")),
        name="s5_scan",
    )(s5u.reshape(bsz, seq, GROUP_W), t_mat, m_mat, p_mat, al_r, al_i)
    return y.reshape(bsz * seq, GROUP_W)


def _moba_kernel(q_ref, k_ref, v_ref, qg_ref, kg_ref, o_ref, kx, vt, kmean, s_a, s_b, acc_s, m_s, *, n_blocks):
    i = pl.program_id(1)
    blk = MOBA_BLOCK
    hd = HEAD_DIM
    seq = n_blocks * blk

    @pl.when(i == 0)
    def _():
        kg = kg_ref[...]
        lane = lax.broadcasted_iota(jnp.int32, (seq, hd), 1)
        kblk = lax.broadcasted_iota(jnp.int32, (seq, hd), 0) // blk
        onehot = jnp.where(lane == kblk, 1.0, 0.0).astype(BF16)
        for h in range(N_HEADS):
            kh = _rms(k_ref[:, h * hd:(h + 1) * hd], kg)
            kx[h, :, 0:hd] = kh.astype(BF16)
            kx[h, :, hd:2 * hd] = onehot
            kmean[h] = jnp.mean(kh.reshape(n_blocks, blk, hd), axis=1)
        ones_rows = jnp.where(lax.broadcasted_iota(jnp.int32, (MOBA_VROWS - hd, blk), 0) == 0, 1.0, 0.0).astype(BF16)
        for nb in range(n_blocks):
            v_t = v_ref[nb * blk:(nb + 1) * blk, :].T
            for h in range(N_HEADS):
                r0 = h * MOBA_VROWS
                vt[r0:r0 + hd, nb * blk:(nb + 1) * blk] = v_t[h * hd:(h + 1) * hd, :].astype(BF16)
                vt[r0 + hd:r0 + MOBA_VROWS, nb * blk:(nb + 1) * blk] = ones_rows

    q_t = q_ref[...].T
    qg = qg_ref[...]
    key_ix = lax.broadcasted_iota(jnp.int32, (blk, blk), 0)
    qry_ix = lax.broadcasted_iota(jnp.int32, (blk, blk), 1)
    causal = key_ix <= qry_ix
    bid = lax.broadcasted_iota(jnp.int32, (n_blocks, blk), 0)
    past = bid < i
    own0 = pl.multiple_of(i * blk, blk)

    qns, qss, s_own = [], [], []
    for h in range(N_HEADS):
        qh = q_t[h * hd:(h + 1) * hd, :]
        qn = qh * lax.rsqrt(jnp.mean(qh * qh, axis=0, keepdims=True) + EPS) * qg
        qs = qn * (hd ** -0.5 * LOG2E)
        qns.append(qn)
        qss.append(qs)
        s_own.append(_dot(kx[h, pl.ds(own0, blk), :],
                          jnp.concatenate([qs, jnp.zeros((hd, blk), F32)], axis=0).astype(BF16)))

    qxs = []
    for h in range(N_HEADS):
        qn = qns[h]
        gate = jnp.dot(kmean[h], qn, preferred_element_type=F32, precision=HIGHEST)
        gate = jnp.where(past, gate, -jnp.inf)
        cnt = jnp.zeros((n_blocks, blk), F32)
        for m in range(n_blocks):
            gm = gate[m:m + 1, :]
            cnt = cnt + jnp.where(gm > gate, 1.0, jnp.where((gm == gate) & (bid > m), 1.0, 0.0))
        bias = jnp.where((bid == i) | (past & (cnt < MOBA_TOPK)), 0.0, NEG)
        qxs.append(jnp.concatenate([qss[h], bias, jnp.zeros((hd - n_blocks, blk), F32)],
                                   axis=0).astype(BF16))

    heads = range(N_HEADS)

    def issue_scores(n, dst):
        k0 = pl.multiple_of(n * blk, blk)
        for h in heads:
            dst[h] = _dot(kx[h, pl.ds(k0, blk), :], qxs[h])

    def values(n, ps):
        k0 = pl.multiple_of(n * blk, blk)
        return [_dot(vt[h * MOBA_VROWS:(h + 1) * MOBA_VROWS, pl.ds(k0, blk)], ps[h]) for h in heads]

    ps = []
    for h in heads:
        s = jnp.where(causal, s_own[h], NEG)
        m0 = jnp.max(s, axis=0, keepdims=True)
        p = jnp.exp2(s - m0)
        m_s[h] = m0
        ps.append(p.astype(BF16))
    for h, pv in zip(heads, values(i, ps)):
        acc_s[h] = pv

    def step(n, src, dst):
        issue_scores(jnp.minimum(n + 1, jnp.maximum(i - 1, 0)), dst)
        ps, alphas = [], []
        for h in heads:
            m_run = m_s[h]
            s = src[h]
            m_new = jnp.maximum(m_run, jnp.max(s, axis=0, keepdims=True))
            alpha = jnp.exp2(m_run - m_new)
            p = jnp.exp2(s - m_new)
            m_s[h] = m_new
            alphas.append(alpha)
            ps.append(p.astype(BF16))
        for h, pv in zip(heads, values(n, ps)):
            acc_s[h] = alphas[h] * acc_s[h] + pv

    issue_scores(0, s_a)

    def pair(k, carry):
        step(2 * k, s_a, s_b)
        step(2 * k + 1, s_b, s_a)
        return carry

    lax.fori_loop(0, i // 2, pair, 0)

    @pl.when(i % 2 == 1)
    def _():
        step(i - 1, s_a, s_b)

    o_ref[...] = jnp.concatenate([acc_s[h, 0:hd, :] / acc_s[h, hd:hd + 1, :] for h in heads],
                                 axis=0).T


def _moba(qkv, q_g, k_g, bsz, seq):
    n_blocks = seq // MOBA_BLOCK
    n = bsz * seq
    return pl.pallas_call(
        functools.partial(_moba_kernel, n_blocks=n_blocks),
        grid=(bsz, n_blocks),
        in_specs=[pl.BlockSpec((MOBA_BLOCK, GROUP_W), lambda b, i: (b * n_blocks + i, 0)),
                  pl.BlockSpec((seq, GROUP_W), lambda b, i: (b, 1)),
                  pl.BlockSpec((seq, GROUP_W), lambda b, i: (b, 2)),
                  pl.BlockSpec((HEAD_DIM, 1), lambda b, i: (0, 0)),
                  pl.BlockSpec((1, HEAD_DIM), lambda b, i: (0, 0))],
        out_specs=pl.BlockSpec((MOBA_BLOCK, GROUP_W), lambda b, i: (b * n_blocks + i, 0)),
        out_shape=jax.ShapeDtypeStruct((n, GROUP_W), F32),
        scratch_shapes=[pltpu.VMEM((N_HEADS, seq, 2 * HEAD_DIM), BF16),
                        pltpu.VMEM((N_HEADS * MOBA_VROWS, seq), BF16),
                        pltpu.VMEM((N_HEADS, n_blocks, HEAD_DIM), F32),
                        pltpu.VMEM((N_HEADS, MOBA_BLOCK, MOBA_BLOCK), F32),
                        pltpu.VMEM((N_HEADS, MOBA_BLOCK, MOBA_BLOCK), F32),
                        pltpu.VMEM((N_HEADS, MOBA_VROWS, MOBA_BLOCK), F32),
                        pltpu.VMEM((N_HEADS, 1, MOBA_BLOCK), F32)],
        compiler_params=_cparams(("parallel", "arbitrary")),
        name="moba",
    )(qkv, qkv, qkv, q_g.reshape(HEAD_DIM, 1), k_g.reshape(1, HEAD_DIM))


def _ssd_kernel(zs_ref, xbc_ref, dt_ref, dtb_ref, a_ref, d_ref, o_ref, state):
    c = pl.program_id(1)
    lc = SSD_CHUNK
    gn = SSD_STATE
    hd = HEAD_DIM
    hpg = N_HEADS // SSD_GROUPS
    chunks = range(SSD_STEP_CHUNKS)
    heads = range(N_HEADS)

    @pl.when(c == 0)
    def _():
        state[...] = jnp.zeros_like(state)

    xbc_all = xbc_ref[...]
    dtr = dt_ref[...] + dtb_ref[...]
    dt_all = jnp.maximum(dtr, 0.0) + jnp.log(1.0 + jnp.exp(-jnp.abs(dtr)))
    adt_all = dt_all * a_ref[...]
    row = lax.broadcasted_iota(jnp.int32, (lc, lc), 0)
    col = lax.broadcasted_iota(jnp.int32, (lc, lc), 1)
    causal = col <= row
    tril = jnp.where(causal, 1.0, 0.0)
    zs_all = zs_ref[...]
    dd = d_ref[...]
    rows = [slice(cc * lc, (cc + 1) * lc) for cc in chunks]
    xbc = [xbc_all[r] for r in rows]
    bmat = [[x[:, GROUP_W + g * gn:GROUP_W + (g + 1) * gn] for g in range(SSD_GROUPS)] for x in xbc]
    cmat = [[x[:, GROUP_W + (SSD_GROUPS + g) * gn:GROUP_W + (SSD_GROUPS + g + 1) * gn].astype(BF16)
             for g in range(SSD_GROUPS)] for x in xbc]
    s_in = [state[h] for h in heads]
    y_off = [[_dot(cmat[0][h // hpg], s_in[h].astype(BF16)) for h in heads]]
    cb = [[_dot_nt(cmat[cc][g], bmat[cc][g].astype(BF16)) for g in range(SSD_GROUPS)] for cc in chunks]
    cs = [jnp.dot(tril, adt_all[r], preferred_element_type=F32, precision=HIGHEST) for r in rows]
    cs_t = [x.T for x in cs]
    y_diag, upd, cs_b, x_h = [], [], [], []
    for cc in chunks:
        yd_c, upd_c, csb_c, xh_c = [], [], [], []
        for h in heads:
            g = h // hpg
            csb = jnp.broadcast_to(cs[cc][:, h:h + 1], (lc, lc))
            seg = csb - cs_t[cc][h:h + 1, :]
            decay = jnp.where(causal, jnp.exp(jnp.where(causal, seg, 0.0)), 0.0)
            scores = (cb[cc][g] * decay).astype(BF16)
            xh = xbc[cc][:, h * hd:(h + 1) * hd]
            xd = (xh * jnp.broadcast_to(dt_all[rows[cc]][:, h:h + 1], (lc, hd))).astype(BF16)
            bd = bmat[cc][g] * jnp.exp(csb[lc - 1:lc, :] - csb)
            yd_c.append(_dot(scores, xd))
            upd_c.append(lax.dot_general(bd.astype(BF16), xd, (((0,), (0,)), ((), ())), preferred_element_type=F32))
            csb_c.append(csb)
            xh_c.append(xh)
        y_diag.append(yd_c); upd.append(upd_c); cs_b.append(csb_c); x_h.append(xh_c)
    s_run = s_in
    for cc in chunks:
        if cc > 0:
            y_off.append([_dot(cmat[cc][h // hpg], s_run[h].astype(BF16)) for h in heads])
        s_run = [jnp.exp(cs_b[cc][h][lc - 1:lc, :hd]) * s_run[h] + upd[cc][h] for h in heads]
    for h in heads:
        state[h] = s_run[h]
    for cc in chunks:
        for h in heads:
            hs = slice(h * hd, (h + 1) * hd)
            y = y_diag[cc][h] + y_off[cc][h] * jnp.exp(cs_b[cc][h][:, :hd]) + dd[:, hs] * x_h[cc][h]
            o_ref[rows[cc], hs] = y * zs_all[rows[cc], hs]


def _ssd(zs, xbc, dt, dt_bias, a_log, d_skip, bsz, seq):
    n = bsz * seq
    rows = SSD_STEP_CHUNKS * SSD_CHUNK
    nc = seq // rows
    dtb = jnp.zeros((1, LANE), F32).at[0, :N_HEADS].set(dt_bias)
    a = jnp.zeros((1, LANE), F32).at[0, :N_HEADS].set(-jnp.exp(a_log))
    dfull = jnp.repeat(d_skip, HEAD_DIM)[None, :]

    def const(shape):
        return pl.BlockSpec(shape, lambda b, c: (0, 0))

    return pl.pallas_call(
        _ssd_kernel,
        grid=(bsz, nc),
        in_specs=[pl.BlockSpec((rows, GROUP_W), lambda b, c: (b * nc + c, 0)),
                  pl.BlockSpec((rows, SSD_CONV_CH), lambda b, c: (b * nc + c, 0)),
                  pl.BlockSpec((rows, LANE), lambda b, c: (b * nc + c, 0)),
                  const((1, LANE)), const((1, LANE)), const((1, GROUP_W))],
        out_specs=pl.BlockSpec((rows, GROUP_W), lambda b, c: (b * nc + c, 0)),
        out_shape=jax.ShapeDtypeStruct((n, GROUP_W), F32),
        scratch_shapes=[pltpu.VMEM((N_HEADS, SSD_STATE, HEAD_DIM), F32)],
        compiler_params=_cparams(("parallel", "arbitrary")),
        name="ssd",
    )(zs, xbc, dt, dtb, a, dfull)


def _gelu_tanh(x):
    return 0.5 * x * (1.0 + jnp.tanh(math.sqrt(2.0 / math.pi) * (x + 0.044715 * (x * x * x))))


def _group_norm(y, g):
    return (y * lax.rsqrt(jnp.mean(y * y, axis=-1, keepdims=True) + EPS) * g).astype(BF16)


def _merge_rows(h, ys5, s5u, att, yc, ssd, s5d, wglu_ref, mg, wo_ref):
    gw = GROUP_W
    ya = ys5 + s5d * s5u
    ga = _gelu_tanh(ya)
    ya = ga * _sigmoid(_dot(ga.astype(BF16), wglu_ref[...]))
    acc = h
    for j, y in enumerate((ya, att, yc, ssd)):
        acc = acc + _dot(_group_norm(y, mg[:, j * gw:(j + 1) * gw]), wo_ref[j * gw:(j + 1) * gw, :])
    return acc


def _tail_kernel(*refs, tm, seq):
    main = refs[0:6]
    halo = refs[6:12]
    (p_ref, s5d_ref, wglu_ref, mg_ref, wo_ref, g_ref, wg_ref, wu_ref, cw_ref, wd_ref,
     pg_ref, pwg_ref, pwp_ref, o_ref) = refs[12:]
    first = (pl.program_id(0) * tm) % seq == 0
    s5d = s5d_ref[...]
    mg = mg_ref[...]
    rm = tm // ROW_SPLIT
    rows = [slice(a * rm, (a + 1) * rm) for a in range(ROW_SPLIT)]
    hs = [_merge_rows(*[r[rw, :] for r in main], s5d, wglu_ref, mg, wo_ref) for rw in rows]
    hh = _merge_rows(*[r[...] for r in halo], s5d, wglu_ref, mg, wo_ref)
    g = g_ref[...]
    vs = [_rms(h, g).astype(BF16) for h in hs]
    vh = _rms(hh, g).astype(BF16)
    ffn = [None] * ROW_SPLIT
    splits = [slice(c0, c1) for c0, c1 in zip(FFN_SPLITS[:-1], FFN_SPLITS[1:])]

    def gate_proj(cs):
        gates = [_dot(v, wg_ref[:, cs]) for v in vs]
        return gates, jnp.where(first, 0.0, _dot(vh, wg_ref[:, cs]))

    nxt = gate_proj(splits[0])
    for ci, cs in enumerate(splits):
        gates, gate_h = nxt
        ups = [_dot(v, wu_ref[:, cs]) for v in vs]
        if ci + 1 < len(splits):
            nxt = gate_proj(splits[ci + 1])
        acts = []
        for a in range(ROW_SPLIT):
            conv = _causal_conv(gates[a], gate_h, cw_ref, FFN_CONV_K, cs)
            acts.append((_silu(conv) * ups[a]).astype(BF16))
            gate_h = gates[a][rm - HALO:, :]
        for a in range(ROW_SPLIT):
            down = _dot(acts[a], wd_ref[cs, :])
            ffn[a] = down if ffn[a] is None else ffn[a] + down
    for a, rw in enumerate(rows):
        acc = hs[a] + ffn[a]
        gate = _sigmoid(_dot(_rms(acc, pg_ref[...]).astype(BF16), pwg_ref[...]))
        o_ref[rw, :] = acc + gate * _dot(p_ref[rw, :].astype(BF16), pwp_ref[...])


def _tail(h, ys5, s5u, att, yc, ssd, p, layer, s5_d, w_glu, merge_g, w_out, ffn_g, w_gate, w_up, conv_w, w_down,
          ple_g, ple_w_gate, ple_w_proj, seq, tm=512):
    n = h.shape[0]
    per = tm // HALO
    widths = (D_MODEL,) + (GROUP_W,) * 5

    def resident(shape):
        return pl.BlockSpec(shape, lambda i: (0, 0), pipeline_mode=pl.Buffered(1))

    def stacked(shape):
        return pl.BlockSpec((None,) + shape, lambda i: (layer, 0, 0), pipeline_mode=pl.Buffered(1))

    return pl.pallas_call(
        functools.partial(_tail_kernel, tm=tm, seq=seq),
        grid=(n // tm,),
        in_specs=([pl.BlockSpec((tm, wd), lambda i: (i, 0)) for wd in widths]
                  + [pl.BlockSpec((HALO, wd), lambda i: (jnp.maximum(i * per - 1, 0), 0)) for wd in widths]
                  + [pl.BlockSpec((None, tm, PLE_DIM), lambda i: (layer, i, 0)),
                     resident((1, GROUP_W)), stacked((GROUP_W, GROUP_W)), resident((1, D_MODEL)),
                     stacked((D_MODEL, D_MODEL)), resident((1, D_MODEL)), stacked((D_MODEL, D_FF)),
                     stacked((D_MODEL, D_FF)), resident((FFN_CONV_K, D_FF)), stacked((D_FF, D_MODEL)),
                     resident((1, D_MODEL)), stacked((D_MODEL, D_MODEL)), stacked((PLE_DIM, D_MODEL))]),
        out_specs=pl.BlockSpec((tm, D_MODEL), lambda i: (i, 0)),
        out_shape=jax.ShapeDtypeStruct((n, D_MODEL), F32),
        compiler_params=_cparams(("parallel",)),
        name="tail",
    )(h, ys5, s5u, att, yc, ssd, h, ys5, s5u, att, yc, ssd, p,
      s5_d[None, :], w_glu, merge_g[None, :], w_out, ffn_g[None, :], w_gate, w_up, conv_w, w_down,
      ple_g[None, :], ple_w_gate, ple_w_proj)


def kernel(x, p, mix_norm_g, w_in, s5_a_re, s5_a_im, s5_log_dt, s5_b_re, s5_b_im, s5_c_re, s5_c_im, s5_d, s5_w_glu, moba_q_g, moba_k_g, sconv_w, ssd_conv_w, ssd_conv_b, ssd_dt_bias, ssd_a_log, ssd_d, merge_norm_g, w_out, ffn_norm_g, ffn_w_gate, ffn_w_up, ffn_conv_w, ffn_w_down, ple_norm_g, ple_w_gate, ple_w_proj):
    bsz, seq, _ = x.shape
    depth = w_in.shape[0]
    n = bsz * seq
    h = x.reshape(n, D_MODEL)
    w_in_b = jnp.pad(w_in, ((0, 0), (0, 0), (0, PROJ_PAD - w_in.shape[2]))).astype(BF16)
    w_glu_b, w_out_b = s5_w_glu.astype(BF16), w_out.astype(BF16)
    w_gate_b, w_up_b, w_down_b = ffn_w_gate.astype(BF16), ffn_w_up.astype(BF16), ffn_w_down.astype(BF16)
    pw_gate_b, pw_proj_b = ple_w_gate.astype(BF16), ple_w_proj.astype(BF16)
    s5_mats = jax.vmap(_s5_param_mats)(s5_a_re, s5_a_im, s5_log_dt, s5_b_re, s5_b_im, s5_c_re, s5_c_im)
    p_rows = p.reshape(depth, n, PLE_DIM)
    for i in range(depth):
        s5u, qkv, yc, zs, xbc, dt = _inproj(h, mix_norm_g[i][None, :], w_in_b, i, sconv_w[i],
                                            ssd_conv_w[i], ssd_conv_b[i], seq)
        ys5 = _s5_scan(s5u, s5_mats, i, bsz, seq)
        att = _moba(qkv, moba_q_g[i][None, :], moba_k_g[i][None, :], bsz, seq)
        yd = _ssd(zs, xbc, dt, ssd_dt_bias[i], ssd_a_log[i], ssd_d[i], bsz, seq)
        h = _tail(h, ys5, s5u, att, yc, yd, p_rows, i, s5_d[i], w_glu_b, merge_norm_g[i], w_out_b,
                  ffn_norm_g[i], w_gate_b, w_up_b, ffn_conv_w[i], w_down_b,
                  ple_norm_g[i], pw_gate_b, pw_proj_b, seq)
    return h.reshape(bsz, seq, D_MODEL)
```

```python
import functools
import math

import jax
import jax.numpy as jnp
from jax import lax
from jax.experimental import pallas as pl
from jax.experimental.pallas import tpu as pltpu

F32 = jnp.float32
BF16 = jnp.bfloat16
EPS = 1e-6
NEG = -1e30
LOG2E = 1.4426950408889634
HIGHEST = lax.Precision.HIGHEST

D_MODEL = 1024
PLE_DIM = 256
GROUP_W = 256
N_MIXERS = 4
S5_CH = 16
S5_GROUPS = 16
S5_STATE = 64
S5_L = 8
S5_HG = 8
HEAD_DIM = 64
N_HEADS = 4
MOBA_BLOCK = 256
MOBA_TOPK = 3
MOBA_VROWS = 80
SSD_GROUPS = 2
SSD_STATE = 128
SSD_CONV_K = 4
SSD_CHUNK = 128
SSD_STEP_CHUNKS = 2
SSD_CONV_CH = GROUP_W + 2 * SSD_GROUPS * SSD_STATE
SCONV_K = 3
D_FF = 2816
FFN_CONV_K = 3
FFN_SPLITS = (0, 1536, 2816)
HALO = 8
INPROJ_ROW_SPLIT = 4
TAIL_ROW_SPLIT = 2
LANE = 128
PROJ_PAD = 2944
VMEM_LIMIT = 56 * 1024 * 1024


def _cparams(sem):
    return pltpu.CompilerParams(dimension_semantics=sem, vmem_limit_bytes=VMEM_LIMIT)


def _rms(x, g):
    return x * lax.rsqrt(jnp.mean(x * x, axis=-1, keepdims=True) + EPS) * g


def _sigmoid(x):
    return 0.5 * jnp.tanh(0.5 * x) + 0.5


def _silu(x):
    return x * _sigmoid(x)


def _dot(a, b):
    return jnp.dot(a, b, preferred_element_type=F32)


def _dot_nt(a, b, precision=None):
    return lax.dot_general(a, b, (((1,), (1,)), ((), ())), preferred_element_type=F32,
                           precision=precision)


def _shift_rows(x, halo, k):
    if k == 0:
        return x
    rolled = pltpu.roll(x, k, axis=0)
    row = lax.broadcasted_iota(jnp.int32, (HALO, x.shape[1]), 0)
    head = jnp.where(row < k, pltpu.roll(halo, k, axis=0), rolled[:HALO])
    return jnp.concatenate([head, rolled[HALO:]], axis=0)


def _causal_conv(x, halo, w_ref, taps, cols=slice(None)):
    acc = w_ref[taps - 1:taps, cols] * x
    for k in range(taps - 1):
        acc = acc + w_ref[k:k + 1, cols] * _shift_rows(x, halo, taps - 1 - k)
    return acc


def _inproj_kernel(h_ref, hh_ref, g_ref, w_ref, scw_ref, cw_ref, cb_ref,
                   s5u_ref, qkv_ref, yc_ref, zs_ref, xbc_ref, dt_ref, *, tm, seq):
    first = (pl.program_id(0) * tm) % seq == 0
    g = g_ref[...]
    rm = tm // INPROJ_ROW_SPLIT
    rows = [slice(a * rm, (a + 1) * rm) for a in range(INPROJ_ROW_SPLIT)]
    u = [_rms(h_ref[r, :], g).astype(BF16) for r in rows]
    uh = _rms(hh_ref[...], g).astype(BF16)

    def proj(c0, c1):
        return [_dot(ua, w_ref[:, c0:c1]) for ua in u]

    def halo_proj(c0, c1):
        return jnp.where(first, 0.0, _dot(uh, w_ref[:, c0:c1]))

    gw = GROUP_W
    sc = proj(1024, 1792)
    sch = halo_proj(1024, 1792)
    z = proj(1792, 2048)
    cx_halo = sch[:, 2 * gw:] * sch[:, :gw]
    for a, r in enumerate(rows):
        cx = sc[a][:, 2 * gw:] * sc[a][:, :gw]
        yc_ref[r, :] = sc[a][:, gw:2 * gw] * _causal_conv(cx, cx_halo, scw_ref, SCONV_K)
        cx_halo = cx[rm - HALO:, :]
    raw = proj(2048, 2816)
    raw_halo = halo_proj(2048, 2816)
    for a, r in enumerate(rows):
        zs_ref[r, :] = _silu(z[a])
    for ref, (c0, c1) in ((s5u_ref, (0, 256)), (qkv_ref, (256, 1024)), (dt_ref, (2816, PROJ_PAD))):
        for r, y in zip(rows, proj(c0, c1)):
            ref[r, :] = y
    for a, r in enumerate(rows):
        xbc_ref[r, :] = _silu(_causal_conv(raw[a], raw_halo, cw_ref, SSD_CONV_K) + cb_ref[...])
        raw_halo = raw[a][rm - HALO:, :]


def _inproj(h, g, w, layer, sconv_w, ssd_conv_w, ssd_conv_b, seq, tm=512):
    n = h.shape[0]
    per = tm // HALO
    widths = (256, 768, 256, 256, 768, LANE)

    def const(shape):
        return pl.BlockSpec(shape, lambda i: (0, 0))

    return pl.pallas_call(
        functools.partial(_inproj_kernel, tm=tm, seq=seq),
        grid=(n // tm,),
        in_specs=[pl.BlockSpec((tm, D_MODEL), lambda i: (i, 0)),
                  pl.BlockSpec((HALO, D_MODEL), lambda i: (jnp.maximum(i * per - 1, 0), 0)),
                  const((1, D_MODEL)), pl.BlockSpec((None, D_MODEL, PROJ_PAD), lambda i: (layer, 0, 0)),
                  const((SCONV_K, GROUP_W)),
                  const((SSD_CONV_K, SSD_CONV_CH)), const((1, SSD_CONV_CH))],
        out_specs=[pl.BlockSpec((tm, wd), lambda i: (i, 0)) for wd in widths],
        out_shape=[jax.ShapeDtypeStruct((n, wd), F32) for wd in widths],
        compiler_params=_cparams(("parallel",)),
        name="inproj",
    )(h, h, g, w, sconv_w, ssd_conv_w, ssd_conv_b[None, :])


def _s5_param_mats(a_re, a_im, log_dt, b_re, b_im, c_re, c_im):
    L = S5_L
    dt = jnp.exp(log_dt)[:, None]
    taus = jnp.arange(L + 1, dtype=F32)[None, :, None]
    mag = jnp.exp((a_re * dt)[:, None, :] * taus)
    ang = (a_im * dt)[:, None, :] * taus
    pw_r, pw_i = mag * jnp.cos(ang), mag * jnp.sin(ang)
    ab_r, ab_i = pw_r[:, 1], pw_i[:, 1]
    den = a_re * a_re + a_im * a_im
    nr = ab_r - 1.0
    coef_r = (nr * a_re + ab_i * a_im) / den
    coef_i = (ab_i * a_re - nr * a_im) / den
    bb_r = coef_r[..., None] * b_re - coef_i[..., None] * b_im
    bb_i = coef_r[..., None] * b_im + coef_i[..., None] * b_re
    rev_r, rev_i = pw_r[:, L - 1::-1], pw_i[:, L - 1::-1]
    m_r = jnp.einsum("gsp,gpc->gscp", rev_r, bb_r) - jnp.einsum("gsp,gpc->gscp", rev_i, bb_i)
    m_i = jnp.einsum("gsp,gpc->gscp", rev_r, bb_i) + jnp.einsum("gsp,gpc->gscp", rev_i, bb_r)
    cp_r = c_re[:, None] * pw_r[:, :, None, :] - c_im[:, None] * pw_i[:, :, None, :]
    cp_i = c_re[:, None] * pw_i[:, :, None, :] + c_im[:, None] * pw_r[:, :, None, :]
    kk = (jnp.einsum("gtcp,gpd->gtcd", cp_r[:, :L], bb_r, precision=HIGHEST)
          - jnp.einsum("gtcp,gpd->gtcd", cp_i[:, :L], bb_i, precision=HIGHEST))
    s_ix = jnp.arange(L)[:, None]
    l_ix = jnp.arange(L)[None, :]
    lag = l_ix - s_ix
    p_r = cp_r[:, 1:].transpose(0, 3, 1, 2)
    p_i = (-cp_i[:, 1:]).transpose(0, 3, 1, 2)
    hg = S5_HG
    nh = S5_GROUPS // hg
    w = L * hg * S5_CH
    ns = hg * S5_STATE
    g_ix = jnp.arange(hg)[:, None, None]
    place_c = (jnp.arange(LANE)[None, None, :] == g_ix * S5_CH + jnp.arange(S5_CH)[None, :, None]).astype(F32)
    place_p = (jnp.arange(ns)[None, None, :] == g_ix * S5_STATE + jnp.arange(S5_STATE)[None, :, None]).astype(F32)

    def halves(a):
        return a.reshape((nh, hg) + a.shape[1:])

    bd = jnp.einsum("gdi,hgtcd,gcj->htij", place_c, halves(kk), place_c, precision=HIGHEST)
    zero_blk = jnp.zeros_like(bd[:, 0])
    t_mat = jnp.concatenate(
        [jnp.concatenate([bd[:, l - s] if l >= s else zero_blk for l in range(L)], axis=-1) for s in range(L)],
        axis=-2)
    m_mat = jnp.concatenate(
        [jnp.einsum("gdi,hgsdp,gpj->hsij", place_c, halves(m), place_p, precision=HIGHEST).reshape(nh, w, ns)
         for m in (m_r, m_i)], axis=-1)
    p_mat = jnp.concatenate(
        [jnp.concatenate(
            [jnp.einsum("hgpc,gcj->hgpj", halves(q)[:, :, :, l], place_c, precision=HIGHEST).reshape(nh, ns, LANE)
             for l in range(L)], axis=-1) for q in (p_r, p_i)], axis=1)
    al_r = pw_r[:, L].reshape(nh, 1, ns)
    al_i = pw_i[:, L].reshape(nh, 1, ns)
    return t_mat.astype(BF16), m_mat.astype(BF16), p_mat.astype(BF16), al_r, al_i


def _s5_kernel(u_ref, t_ref, m_ref, p_ref, alr_ref, ali_ref, y_ref, loc, sp, sr_s, si_s, *, steps, bsz):
    ns = S5_HG * S5_STATE

    @pl.when(pl.program_id(1) == 0)
    def _():
        sr_s[...] = jnp.zeros_like(sr_s)
        si_s[...] = jnp.zeros_like(si_s)

    nt = ns // LANE
    u = jnp.concatenate([u_ref[:, pl.ds(l, steps, stride=S5_L), :].reshape(bsz * steps, LANE)
                         for l in range(S5_L)], axis=1).astype(BF16)
    loc_v = _dot(u, m_ref[0])
    for j in range(2 * nt):
        loc[j] = loc_v[:, j * LANE:(j + 1) * LANE]
    ar = [jnp.broadcast_to(alr_ref[0, :, j * LANE:(j + 1) * LANE], (bsz, LANE)) for j in range(nt)]
    ai = [jnp.broadcast_to(ali_ref[0, :, j * LANE:(j + 1) * LANE], (bsz, LANE)) for j in range(nt)]

    def step(k, carry):
        sr, si = carry
        rows = pl.ds(k, bsz, stride=steps)
        nr, ni = [], []
        for j in range(nt):
            sp[j, rows, :] = sr[j]
            sp[nt + j, rows, :] = si[j]
            nr.append(ar[j] * sr[j] - ai[j] * si[j] + loc[j, rows, :])
            ni.append(ar[j] * si[j] + ai[j] * sr[j] + loc[nt + j, rows, :])
        return tuple(nr), tuple(ni)

    init = (tuple(sr_s[:, j * LANE:(j + 1) * LANE] for j in range(nt)),
            tuple(si_s[:, j * LANE:(j + 1) * LANE] for j in range(nt)))
    sr, si = lax.fori_loop(0, steps, step, init, unroll=4)
    for j in range(nt):
        sr_s[:, j * LANE:(j + 1) * LANE] = sr[j]
        si_s[:, j * LANE:(j + 1) * LANE] = si[j]
    s_prev = jnp.concatenate([sp[j] for j in range(2 * nt)], axis=1).astype(BF16)
    y = _dot(u, t_ref[0]) + _dot(s_prev, p_ref[0])
    for l in range(S5_L):
        y_ref[:, pl.ds(l, steps, stride=S5_L), :] = y[:, l * LANE:(l + 1) * LANE].reshape(bsz, steps, LANE)


def _s5_scan(s5u, mats, layer, bsz, seq, steps=64):
    n_chunks = seq // S5_L
    w = S5_L * LANE
    ns = S5_HG * S5_STATE
    nh = GROUP_W // LANE
    t_mat, m_mat, p_mat, al_r, al_i = mats

    def hspec(a, b):
        return pl.BlockSpec((None, 1, a, b), lambda hf, j: (layer, hf, 0, 0))

    rspec = pl.BlockSpec((bsz, steps * S5_L, LANE), lambda hf, j: (0, j, hf))
    y = pl.pallas_call(
        functools.partial(_s5_kernel, steps=steps, bsz=bsz),
        grid=(nh, n_chunks // steps),
        in_specs=[rspec, hspec(w, w), hspec(w, 2 * ns), hspec(2 * ns, w), hspec(1, ns), hspec(1, ns)],
        out_specs=rspec,
        out_shape=jax.ShapeDtypeStruct((bsz, seq, GROUP_W), F32),
        scratch_shapes=[pltpu.VMEM((2 * ns // LANE, bsz * steps, LANE), F32),
                        pltpu.VMEM((2 * ns // LANE, bsz * steps, LANE), F32),
                        pltpu.VMEM((bsz, ns), F32), pltpu.VMEM((bsz, ns), F32)],
        compiler_params=_cparams(("parallel", "arbitrary")),
        name="s5_scan",
    )(s5u.reshape(bsz, seq, GROUP_W), t_mat, m_mat, p_mat, al_r, al_i)
    return y.reshape(bsz * seq, GROUP_W)


def _moba_kernel(q_ref, k_ref, v_ref, qg_ref, kg_ref, o_ref, kx, vt, kmean, s_a, s_b, acc_s, m_s, *, n_blocks):
    i = pl.program_id(1)
    blk = MOBA_BLOCK
    hd = HEAD_DIM
    seq = n_blocks * blk

    @pl.when(i == 0)
    def _():
        kg = kg_ref[...]
        lane = lax.broadcasted_iota(jnp.int32, (seq, hd), 1)
        kblk = lax.broadcasted_iota(jnp.int32, (seq, hd), 0) // blk
        onehot = jnp.where(lane == kblk, 1.0, 0.0).astype(BF16)
        for h in range(N_HEADS):
            kh = _rms(k_ref[:, h * hd:(h + 1) * hd], kg)
            kx[h, :, 0:hd] = kh.astype(BF16)
            kx[h, :, hd:2 * hd] = onehot
            kmean[h] = jnp.mean(kh.reshape(n_blocks, blk, hd), axis=1)
        ones_rows = jnp.where(lax.broadcasted_iota(jnp.int32, (MOBA_VROWS - hd, blk), 0) == 0, 1.0, 0.0).astype(BF16)
        for nb in range(n_blocks):
            v_t = v_ref[nb * blk:(nb + 1) * blk, :].T
            for h in range(N_HEADS):
                r0 = h * MOBA_VROWS
                vt[r0:r0 + hd, nb * blk:(nb + 1) * blk] = v_t[h * hd:(h + 1) * hd, :].astype(BF16)
                vt[r0 + hd:r0 + MOBA_VROWS, nb * blk:(nb + 1) * blk] = ones_rows

    q_t = q_ref[...].T
    qg = qg_ref[...]
    key_ix = lax.broadcasted_iota(jnp.int32, (blk, blk), 0)
    qry_ix = lax.broadcasted_iota(jnp.int32, (blk, blk), 1)
    causal = key_ix <= qry_ix
    bid = lax.broadcasted_iota(jnp.int32, (n_blocks, blk), 0)
    past = bid < i
    own0 = pl.multiple_of(i * blk, blk)

    qns, qss, s_own = [], [], []
    for h in range(N_HEADS):
        qh = q_t[h * hd:(h + 1) * hd, :]
        qn = qh * lax.rsqrt(jnp.mean(qh * qh, axis=0, keepdims=True) + EPS) * qg
        qs = qn * (hd ** -0.5 * LOG2E)
        qns.append(qn)
        qss.append(qs)
        s_own.append(_dot(kx[h, pl.ds(own0, blk), :],
                          jnp.concatenate([qs, jnp.zeros((hd, blk), F32)], axis=0).astype(BF16)))

    ties = [jnp.where(bid > m, 1.0, 0.0) for m in range(n_blocks)]
    qxs = []
    for h in range(N_HEADS):
        qn = qns[h]
        gate = jnp.dot(kmean[h], qn, preferred_element_type=F32, precision=HIGHEST)
        gate = jnp.where(past, gate, -jnp.inf)
        cnt = jnp.zeros((n_blocks, blk), F32)
        for m in range(n_blocks):
            gm = gate[m:m + 1, :]
            cnt = cnt + jnp.where(gm > gate, 1.0, jnp.where(gm == gate, ties[m], 0.0))
        bias = jnp.where((bid == i) | (past & (cnt < MOBA_TOPK)), 0.0, NEG)
        qxs.append(jnp.concatenate([qss[h], bias, jnp.zeros((hd - n_blocks, blk), F32)],
                                   axis=0).astype(BF16))

    heads = range(N_HEADS)

    def issue_scores(n, dst):
        k0 = pl.multiple_of(n * blk, blk)
        for h in heads:
            dst[h] = _dot(kx[h, pl.ds(k0, blk), :], qxs[h])

    def values(n, ps):
        k0 = pl.multiple_of(n * blk, blk)
        return [_dot(vt[h * MOBA_VROWS:(h + 1) * MOBA_VROWS, pl.ds(k0, blk)], ps[h]) for h in heads]

    ps = []
    for h in heads:
        s = jnp.where(causal, s_own[h], NEG)
        m0 = jnp.max(s, axis=0, keepdims=True)
        p = jnp.exp2(s - m0)
        m_s[h] = m0
        ps.append(p.astype(BF16))
    for h, pv in zip(heads, values(i, ps)):
        acc_s[h] = pv

    def step(n, src, dst):
        issue_scores(jnp.minimum(n + 1, jnp.maximum(i - 1, 0)), dst)
        ps, alphas = [], []
        for h in heads:
            m_run = m_s[h]
            s = src[h]
            m_new = jnp.maximum(m_run, jnp.max(s, axis=0, keepdims=True))
            alpha = jnp.exp2(m_run - m_new)
            p = jnp.exp2(s - m_new)
            m_s[h] = m_new
            alphas.append(alpha)
            ps.append(p.astype(BF16))
        for h, pv in zip(heads, values(n, ps)):
            acc_s[h] = alphas[h] * acc_s[h] + pv

    issue_scores(0, s_a)

    def pair(k, carry):
        step(2 * k, s_a, s_b)
        step(2 * k + 1, s_b, s_a)
        return carry

    lax.fori_loop(0, i // 2, pair, 0)

    @pl.when(i % 2 == 1)
    def _():
        step(i - 1, s_a, s_b)

    o_ref[...] = jnp.concatenate([acc_s[h, 0:hd, :] / acc_s[h, hd:hd + 1, :] for h in heads],
                                 axis=0).T


def _moba(qkv, q_g, k_g, bsz, seq):
    n_blocks = seq // MOBA_BLOCK
    n = bsz * seq
    return pl.pallas_call(
        functools.partial(_moba_kernel, n_blocks=n_blocks),
        grid=(bsz, n_blocks),
        in_specs=[pl.BlockSpec((MOBA_BLOCK, GROUP_W), lambda b, i: (b * n_blocks + i, 0)),
                  pl.BlockSpec((seq, GROUP_W), lambda b, i: (b, 1)),
                  pl.BlockSpec((seq, GROUP_W), lambda b, i: (b, 2)),
                  pl.BlockSpec((HEAD_DIM, 1), lambda b, i: (0, 0)),
                  pl.BlockSpec((1, HEAD_DIM), lambda b, i: (0, 0))],
        out_specs=pl.BlockSpec((MOBA_BLOCK, GROUP_W), lambda b, i: (b * n_blocks + i, 0)),
        out_shape=jax.ShapeDtypeStruct((n, GROUP_W), F32),
        scratch_shapes=[pltpu.VMEM((N_HEADS, seq, 2 * HEAD_DIM), BF16),
                        pltpu.VMEM((N_HEADS * MOBA_VROWS, seq), BF16),
                        pltpu.VMEM((N_HEADS, n_blocks, HEAD_DIM), F32),
                        pltpu.VMEM((N_HEADS, MOBA_BLOCK, MOBA_BLOCK), F32),
                        pltpu.VMEM((N_HEADS, MOBA_BLOCK, MOBA_BLOCK), F32),
                        pltpu.VMEM((N_HEADS, MOBA_VROWS, MOBA_BLOCK), F32),
                        pltpu.VMEM((N_HEADS, 1, MOBA_BLOCK), F32)],
        compiler_params=_cparams(("parallel", "arbitrary")),
        name="moba",
    )(qkv, qkv, qkv, q_g.reshape(HEAD_DIM, 1), k_g.reshape(1, HEAD_DIM))


def _ssd_kernel(zs_ref, xbc_ref, dt_ref, dtb_ref, a_ref, d_ref, o_ref, state):
    c = pl.program_id(1)
    lc = SSD_CHUNK
    gn = SSD_STATE
    hd = HEAD_DIM
    hpg = N_HEADS // SSD_GROUPS
    chunks = range(SSD_STEP_CHUNKS)
    heads = range(N_HEADS)

    @pl.when(c == 0)
    def _():
        state[...] = jnp.zeros_like(state)

    xbc_all = xbc_ref[...]
    dtr = dt_ref[...] + dtb_ref[...]
    dt_all = jnp.maximum(dtr, 0.0) + jnp.log(1.0 + jnp.exp(-jnp.abs(dtr)))
    adt_all = dt_all * a_ref[...]
    row = lax.broadcasted_iota(jnp.int32, (lc, lc), 0)
    col = lax.broadcasted_iota(jnp.int32, (lc, lc), 1)
    causal = col <= row
    tril = jnp.where(causal, 1.0, 0.0)
    zs_all = zs_ref[...]
    dd = d_ref[...]
    rows = [slice(cc * lc, (cc + 1) * lc) for cc in chunks]
    xbc = [xbc_all[r] for r in rows]
    bmat = [[x[:, GROUP_W + g * gn:GROUP_W + (g + 1) * gn] for g in range(SSD_GROUPS)] for x in xbc]
    cmat = [[x[:, GROUP_W + (SSD_GROUPS + g) * gn:GROUP_W + (SSD_GROUPS + g + 1) * gn].astype(BF16)
             for g in range(SSD_GROUPS)] for x in xbc]
    s_in = [state[h] for h in heads]
    y_off = [[_dot(cmat[0][h // hpg], s_in[h].astype(BF16)) for h in heads]]
    cb = [[_dot_nt(cmat[cc][g], bmat[cc][g].astype(BF16)) for g in range(SSD_GROUPS)] for cc in chunks]
    cs = [jnp.dot(tril, adt_all[r], preferred_element_type=F32, precision=HIGHEST) for r in rows]
    cs_t = [x.T for x in cs]
    y_diag, upd, cs_b, x_h = [], [], [], []
    for cc in chunks:
        yd_c, upd_c, csb_c, xh_c = [], [], [], []
        for h in heads:
            g = h // hpg
            csb = jnp.broadcast_to(cs[cc][:, h:h + 1], (lc, lc))
            seg = csb - cs_t[cc][h:h + 1, :]
            decay = jnp.where(causal, jnp.exp(jnp.where(causal, seg, 0.0)), 0.0)
            scores = (cb[cc][g] * decay).astype(BF16)
            xh = xbc[cc][:, h * hd:(h + 1) * hd]
            xd = (xh * jnp.broadcast_to(dt_all[rows[cc]][:, h:h + 1], (lc, hd))).astype(BF16)
            bd = bmat[cc][g] * jnp.exp(csb[lc - 1:lc, :] - csb)
            yd_c.append(_dot(scores, xd))
            upd_c.append(lax.dot_general(bd.astype(BF16), xd, (((0,), (0,)), ((), ())), preferred_element_type=F32))
            csb_c.append(csb)
            xh_c.append(xh)
        y_diag.append(yd_c); upd.append(upd_c); cs_b.append(csb_c); x_h.append(xh_c)
    s_run = s_in
    for cc in chunks:
        if cc > 0:
            y_off.append([_dot(cmat[cc][h // hpg], s_run[h].astype(BF16)) for h in heads])
        s_run = [jnp.exp(cs_b[cc][h][lc - 1:lc, :hd]) * s_run[h] + upd[cc][h] for h in heads]
    for h in heads:
        state[h] = s_run[h]
    for cc in chunks:
        for h in heads:
            hs = slice(h * hd, (h + 1) * hd)
            y = y_diag[cc][h] + y_off[cc][h] * jnp.exp(cs_b[cc][h][:, :hd]) + dd[:, hs] * x_h[cc][h]
            o_ref[rows[cc], hs] = y * zs_all[rows[cc], hs]


def _ssd(zs, xbc, dt, dt_bias, a_log, d_skip, bsz, seq):
    n = bsz * seq
    rows = SSD_STEP_CHUNKS * SSD_CHUNK
    nc = seq // rows
    dtb = jnp.zeros((1, LANE), F32).at[0, :N_HEADS].set(dt_bias)
    a = jnp.zeros((1, LANE), F32).at[0, :N_HEADS].set(-jnp.exp(a_log))
    dfull = jnp.repeat(d_skip, HEAD_DIM)[None, :]

    def const(shape):
        return pl.BlockSpec(shape, lambda b, c: (0, 0))

    return pl.pallas_call(
        _ssd_kernel,
        grid=(bsz, nc),
        in_specs=[pl.BlockSpec((rows, GROUP_W), lambda b, c: (b * nc + c, 0)),
                  pl.BlockSpec((rows, SSD_CONV_CH), lambda b, c: (b * nc + c, 0)),
                  pl.BlockSpec((rows, LANE), lambda b, c: (b * nc + c, 0)),
                  const((1, LANE)), const((1, LANE)), const((1, GROUP_W))],
        out_specs=pl.BlockSpec((rows, GROUP_W), lambda b, c: (b * nc + c, 0)),
        out_shape=jax.ShapeDtypeStruct((n, GROUP_W), F32),
        scratch_shapes=[pltpu.VMEM((N_HEADS, SSD_STATE, HEAD_DIM), F32)],
        compiler_params=_cparams(("parallel", "arbitrary")),
        name="ssd",
    )(zs, xbc, dt, dtb, a, dfull)


def _gelu_tanh(x):
    return 0.5 * x * (1.0 + jnp.tanh(math.sqrt(2.0 / math.pi) * (x + 0.044715 * (x * x * x))))


def _group_norm(y, g):
    return (y * lax.rsqrt(jnp.mean(y * y, axis=-1, keepdims=True) + EPS) * g).astype(BF16)


def _merge_rows(h, ys5, s5u, att, yc, ssd, s5d, wglu_ref, mg, wo_ref):
    gw = GROUP_W
    ya = ys5 + s5d * s5u
    ga = _gelu_tanh(ya)
    ya = ga * _sigmoid(_dot(ga.astype(BF16), wglu_ref[...]))
    acc = h
    for j, y in enumerate((ya, att, yc, ssd)):
        acc = acc + _dot(_group_norm(y, mg[:, j * gw:(j + 1) * gw]), wo_ref[j * gw:(j + 1) * gw, :])
    return acc


def _tail_kernel(*refs, tm, seq):
    main = refs[0:6]
    halo = refs[6:12]
    (p_ref, s5d_ref, wglu_ref, mg_ref, wo_ref, g_ref, wg_ref, wu_ref, cw_ref, wd_ref,
     pg_ref, pwg_ref, pwp_ref, o_ref) = refs[12:]
    first = (pl.program_id(0) * tm) % seq == 0
    s5d = s5d_ref[...]
    mg = mg_ref[...]
    rm = tm // TAIL_ROW_SPLIT
    rows = [slice(a * rm, (a + 1) * rm) for a in range(TAIL_ROW_SPLIT)]
    hs = [_merge_rows(*[r[rw, :] for r in main], s5d, wglu_ref, mg, wo_ref) for rw in rows]
    hh = _merge_rows(*[r[...] for r in halo], s5d, wglu_ref, mg, wo_ref)
    g = g_ref[...]
    vs = [_rms(h, g).astype(BF16) for h in hs]
    vh = _rms(hh, g).astype(BF16)
    ffn = [None] * TAIL_ROW_SPLIT
    splits = [slice(c0, c1) for c0, c1 in zip(FFN_SPLITS[:-1], FFN_SPLITS[1:])]

    def gate_proj(cs):
        gates = [_dot(v, wg_ref[:, cs]) for v in vs]
        return gates, jnp.where(first, 0.0, _dot(vh, wg_ref[:, cs]))

    nxt = gate_proj(splits[0])
    for ci, cs in enumerate(splits):
        gates, gate_h = nxt
        ups = [_dot(v, wu_ref[:, cs]) for v in vs]
        if ci + 1 < len(splits):
            nxt = gate_proj(splits[ci + 1])
        acts = []
        for a in range(TAIL_ROW_SPLIT):
            conv = _causal_conv(gates[a], gate_h, cw_ref, FFN_CONV_K, cs)
            acts.append((_silu(conv) * ups[a]).astype(BF16))
            gate_h = gates[a][rm - HALO:, :]
        for a in range(TAIL_ROW_SPLIT):
            down = _dot(acts[a], wd_ref[cs, :])
            ffn[a] = down if ffn[a] is None else ffn[a] + down
    for a, rw in enumerate(rows):
        acc = hs[a] + ffn[a]
        gate = _sigmoid(_dot(_rms(acc, pg_ref[...]).astype(BF16), pwg_ref[...]))
        o_ref[rw, :] = acc + gate * _dot(p_ref[rw, :].astype(BF16), pwp_ref[...])


def _tail(h, ys5, s5u, att, yc, ssd, p, layer, s5_d, w_glu, merge_g, w_out, ffn_g, w_gate, w_up, conv_w, w_down,
          ple_g, ple_w_gate, ple_w_proj, seq, tm=512):
    n = h.shape[0]
    per = tm // HALO
    widths = (D_MODEL,) + (GROUP_W,) * 5

    def resident(shape):
        return pl.BlockSpec(shape, lambda i: (0, 0), pipeline_mode=pl.Buffered(1))

    def stacked(shape):
        return pl.BlockSpec((None,) + shape, lambda i: (layer, 0, 0), pipeline_mode=pl.Buffered(1))

    return pl.pallas_call(
        functools.partial(_tail_kernel, tm=tm, seq=seq),
        grid=(n // tm,),
        in_specs=([pl.BlockSpec((tm, wd), lambda i: (i, 0)) for wd in widths]
                  + [pl.BlockSpec((HALO, wd), lambda i: (jnp.maximum(i * per - 1, 0), 0)) for wd in widths]
                  + [pl.BlockSpec((None, tm, PLE_DIM), lambda i: (layer, i, 0)),
                     resident((1, GROUP_W)), stacked((GROUP_W, GROUP_W)), resident((1, D_MODEL)),
                     stacked((D_MODEL, D_MODEL)), resident((1, D_MODEL)), stacked((D_MODEL, D_FF)),
                     stacked((D_MODEL, D_FF)), resident((FFN_CONV_K, D_FF)), stacked((D_FF, D_MODEL)),
                     resident((1, D_MODEL)), stacked((D_MODEL, D_MODEL)), stacked((PLE_DIM, D_MODEL))]),
        out_specs=pl.BlockSpec((tm, D_MODEL), lambda i: (i, 0)),
        out_shape=jax.ShapeDtypeStruct((n, D_MODEL), F32),
        compiler_params=_cparams(("parallel",)),
        name="tail",
    )(h, ys5, s5u, att, yc, ssd, h, ys5, s5u, att, yc, ssd, p,
      s5_d[None, :], w_glu, merge_g[None, :], w_out, ffn_g[None, :], w_gate, w_up, conv_w, w_down,
      ple_g[None, :], ple_w_gate, ple_w_proj)


def kernel(x, p, mix_norm_g, w_in, s5_a_re, s5_a_im, s5_log_dt, s5_b_re, s5_b_im, s5_c_re, s5_c_im, s5_d, s5_w_glu, moba_q_g, moba_k_g, sconv_w, ssd_conv_w, ssd_conv_b, ssd_dt_bias, ssd_a_log, ssd_d, merge_norm_g, w_out, ffn_norm_g, ffn_w_gate, ffn_w_up, ffn_conv_w, ffn_w_down, ple_norm_g, ple_w_gate, ple_w_proj):
    bsz, seq, _ = x.shape
    depth = w_in.shape[0]
    n = bsz * seq
    h = x.reshape(n, D_MODEL)
    w_in_b = jnp.pad(w_in, ((0, 0), (0, 0), (0, PROJ_PAD - w_in.shape[2]))).astype(BF16)
    w_glu_b, w_out_b = s5_w_glu.astype(BF16), w_out.astype(BF16)
    w_gate_b, w_up_b, w_down_b = ffn_w_gate.astype(BF16), ffn_w_up.astype(BF16), ffn_w_down.astype(BF16)
    pw_gate_b, pw_proj_b = ple_w_gate.astype(BF16), ple_w_proj.astype(BF16)
    s5_mats = jax.vmap(_s5_param_mats)(s5_a_re, s5_a_im, s5_log_dt, s5_b_re, s5_b_im, s5_c_re, s5_c_im)
    p_rows = p.reshape(depth, n, PLE_DIM)
    for i in range(depth):
        s5u, qkv, yc, zs, xbc, dt = _inproj(h, mix_norm_g[i][None, :], w_in_b, i, sconv_w[i],
                                            ssd_conv_w[i], ssd_conv_b[i], seq)
        ys5 = _s5_scan(s5u, s5_mats, i, bsz, seq)
        att = _moba(qkv, moba_q_g[i][None, :], moba_k_g[i][None, :], bsz, seq)
        yd = _ssd(zs, xbc, dt, ssd_dt_bias[i], ssd_a_log[i], ssd_d[i], bsz, seq)
        h = _tail(h, ys5, s5u, att, yc, yd, p_rows, i, s5_d[i], w_glu_b, merge_norm_g[i], w_out_b,
                  ffn_norm_g[i], w_gate_b, w_up_b, ffn_conv_w[i], w_down_b,
                  ple_norm_g[i], pw_gate_b, pw_proj_b, seq)
    return h.reshape(bsz, seq, D_MODEL)
```

```python
import functools
import math

import jax
import jax.numpy as jnp
from jax import lax
from jax.experimental import pallas as pl
from jax.experimental.pallas import tpu as pltpu

F32 = jnp.float32
BF16 = jnp.bfloat16
EPS = 1e-6
NEG = -1e30
LOG2E = 1.4426950408889634
HIGHEST = lax.Precision.HIGHEST

D_MODEL = 1024
PLE_DIM = 256
GROUP_W = 256
N_MIXERS = 4
S5_CH = 16
S5_GROUPS = 16
S5_STATE = 64
S5_L = 8
S5_HG = 8
HEAD_DIM = 64
N_HEADS = 4
MOBA_BLOCK = 256
MOBA_TOPK = 3
MOBA_VROWS = 80
SSD_GROUPS = 2
SSD_STATE = 128
SSD_CONV_K = 4
SSD_CHUNK = 128
SSD_STEP_CHUNKS = 2
SSD_CONV_CH = GROUP_W + 2 * SSD_GROUPS * SSD_STATE
SCONV_K = 3
D_FF = 2816
FFN_CONV_K = 3
FFN_SPLITS = (0, 1536, 2816)
HALO = 8
ROW_SPLIT = 2
LANE = 128
PROJ_PAD = 2944
VMEM_LIMIT = 56 * 1024 * 1024


def _cparams(sem):
    return pltpu.CompilerParams(dimension_semantics=sem, vmem_limit_bytes=VMEM_LIMIT)


def _rms(x, g):
    return x * lax.rsqrt(jnp.mean(x * x, axis=-1, keepdims=True) + EPS) * g


def _sigmoid(x):
    return 0.5 * jnp.tanh(0.5 * x) + 0.5


def _silu(x):
    return x * _sigmoid(x)


def _dot(a, b):
    return jnp.dot(a, b, preferred_element_type=F32)


def _dot_nt(a, b, precision=None):
    return lax.dot_general(a, b, (((1,), (1,)), ((), ())), preferred_element_type=F32,
                           precision=precision)


def _shift_rows(x, halo, k):
    if k == 0:
        return x
    rolled = pltpu.roll(x, k, axis=0)
    row = lax.broadcasted_iota(jnp.int32, (HALO, x.shape[1]), 0)
    head = jnp.where(row < k, pltpu.roll(halo, k, axis=0), rolled[:HALO])
    return jnp.concatenate([head, rolled[HALO:]], axis=0)


def _causal_conv(x, halo, w_ref, taps, cols=slice(None)):
    acc = w_ref[taps - 1:taps, cols] * x
    for k in range(taps - 1):
        acc = acc + w_ref[k:k + 1, cols] * _shift_rows(x, halo, taps - 1 - k)
    return acc


def _inproj_kernel(h_ref, hh_ref, g_ref, w_ref, scw_ref, cw_ref, cb_ref,
                   s5u_ref, qkv_ref, yc_ref, zs_ref, xbc_ref, dt_ref, *, tm, seq):
    first = (pl.program_id(0) * tm) % seq == 0
    g = g_ref[...]
    rm = tm // ROW_SPLIT
    rows = [slice(a * rm, (a + 1) * rm) for a in range(ROW_SPLIT)]
    u = [_rms(h_ref[r, :], g).astype(BF16) for r in rows]
    uh = _rms(hh_ref[...], g).astype(BF16)

    def proj(c0, c1):
        return [_dot(ua, w_ref[:, c0:c1]) for ua in u]

    def halo_proj(c0, c1):
        return jnp.where(first, 0.0, _dot(uh, w_ref[:, c0:c1]))

    gw = GROUP_W
    sc = proj(1024, 1792)
    sch = halo_proj(1024, 1792)
    z = proj(1792, 2048)
    cx_halo = sch[:, 2 * gw:] * sch[:, :gw]
    for a, r in enumerate(rows):
        cx = sc[a][:, 2 * gw:] * sc[a][:, :gw]
        yc_ref[r, :] = sc[a][:, gw:2 * gw] * _causal_conv(cx, cx_halo, scw_ref, SCONV_K)
        cx_halo = cx[rm - HALO:, :]
    raw = proj(2048, 2816)
    raw_halo = halo_proj(2048, 2816)
    for a, r in enumerate(rows):
        zs_ref[r, :] = _silu(z[a])
    for ref, (c0, c1) in ((s5u_ref, (0, 256)), (qkv_ref, (256, 1024)), (dt_ref, (2816, PROJ_PAD))):
        for r, y in zip(rows, proj(c0, c1)):
            ref[r, :] = y
    for a, r in enumerate(rows):
        xbc_ref[r, :] = _silu(_causal_conv(raw[a], raw_halo, cw_ref, SSD_CONV_K) + cb_ref[...])
        raw_halo = raw[a][rm - HALO:, :]


def _inproj(h, g, w, layer, sconv_w, ssd_conv_w, ssd_conv_b, seq, tm=512):
    n = h.shape[0]
    per = tm // HALO
    widths = (256, 768, 256, 256, 768, LANE)

    def const(shape):
        return pl.BlockSpec(shape, lambda i: (0, 0))

    return pl.pallas_call(
        functools.partial(_inproj_kernel, tm=tm, seq=seq),
        grid=(n // tm,),
        in_specs=[pl.BlockSpec((tm, D_MODEL), lambda i: (i, 0)),
                  pl.BlockSpec((HALO, D_MODEL), lambda i: (jnp.maximum(i * per - 1, 0), 0)),
                  const((1, D_MODEL)), pl.BlockSpec((None, D_MODEL, PROJ_PAD), lambda i: (layer, 0, 0)),
                  const((SCONV_K, GROUP_W)),
                  const((SSD_CONV_K, SSD_CONV_CH)), const((1, SSD_CONV_CH))],
        out_specs=[pl.BlockSpec((tm, wd), lambda i: (i, 0)) for wd in widths],
        out_shape=[jax.ShapeDtypeStruct((n, wd), F32) for wd in widths],
        compiler_params=_cparams(("parallel",)),
        name="inproj",
    )(h, h, g, w, sconv_w, ssd_conv_w, ssd_conv_b[None, :])


def _s5_param_mats(a_re, a_im, log_dt, b_re, b_im, c_re, c_im):
    L = S5_L
    dt = jnp.exp(log_dt)[:, None]
    taus = jnp.arange(L + 1, dtype=F32)[None, :, None]
    mag = jnp.exp((a_re * dt)[:, None, :] * taus)
    ang = (a_im * dt)[:, None, :] * taus
    pw_r, pw_i = mag * jnp.cos(ang), mag * jnp.sin(ang)
    ab_r, ab_i = pw_r[:, 1], pw_i[:, 1]
    den = a_re * a_re + a_im * a_im
    nr = ab_r - 1.0
    coef_r = (nr * a_re + ab_i * a_im) / den
    coef_i = (ab_i * a_re - nr * a_im) / den
    bb_r = coef_r[..., None] * b_re - coef_i[..., None] * b_im
    bb_i = coef_r[..., None] * b_im + coef_i[..., None] * b_re
    rev_r, rev_i = pw_r[:, L - 1::-1], pw_i[:, L - 1::-1]
    m_r = jnp.einsum("gsp,gpc->gscp", rev_r, bb_r) - jnp.einsum("gsp,gpc->gscp", rev_i, bb_i)
    m_i = jnp.einsum("gsp,gpc->gscp", rev_r, bb_i) + jnp.einsum("gsp,gpc->gscp", rev_i, bb_r)
    cp_r = c_re[:, None] * pw_r[:, :, None, :] - c_im[:, None] * pw_i[:, :, None, :]
    cp_i = c_re[:, None] * pw_i[:, :, None, :] + c_im[:, None] * pw_r[:, :, None, :]
    kk = (jnp.einsum("gtcp,gpd->gtcd", cp_r[:, :L], bb_r, precision=HIGHEST)
          - jnp.einsum("gtcp,gpd->gtcd", cp_i[:, :L], bb_i, precision=HIGHEST))
    s_ix = jnp.arange(L)[:, None]
    l_ix = jnp.arange(L)[None, :]
    lag = l_ix - s_ix
    p_r = cp_r[:, 1:].transpose(0, 3, 1, 2)
    p_i = (-cp_i[:, 1:]).transpose(0, 3, 1, 2)
    hg = S5_HG
    nh = S5_GROUPS // hg
    w = L * hg * S5_CH
    ns = hg * S5_STATE
    g_ix = jnp.arange(hg)[:, None, None]
    place_c = (jnp.arange(LANE)[None, None, :] == g_ix * S5_CH + jnp.arange(S5_CH)[None, :, None]).astype(F32)
    place_p = (jnp.arange(ns)[None, None, :] == g_ix * S5_STATE + jnp.arange(S5_STATE)[None, :, None]).astype(F32)

    def halves(a):
        return a.reshape((nh, hg) + a.shape[1:])

    bd = jnp.einsum("gdi,hgtcd,gcj->htij", place_c, halves(kk), place_c, precision=HIGHEST)
    zero_blk = jnp.zeros_like(bd[:, 0])
    t_mat = jnp.concatenate(
        [jnp.concatenate([bd[:, l - s] if l >= s else zero_blk for l in range(L)], axis=-1) for s in range(L)],
        axis=-2)
    m_mat = jnp.concatenate(
        [jnp.einsum("gdi,hgsdp,gpj->hsij", place_c, halves(m), place_p, precision=HIGHEST).reshape(nh, w, ns)
         for m in (m_r, m_i)], axis=-1)
    p_mat = jnp.concatenate(
        [jnp.concatenate(
            [jnp.einsum("hgpc,gcj->hgpj", halves(q)[:, :, :, l], place_c, precision=HIGHEST).reshape(nh, ns, LANE)
             for l in range(L)], axis=-1) for q in (p_r, p_i)], axis=1)
    al_r = pw_r[:, L].reshape(nh, 1, ns)
    al_i = pw_i[:, L].reshape(nh, 1, ns)
    return t_mat.astype(BF16), m_mat.astype(BF16), p_mat.astype(BF16), al_r, al_i


def _s5_kernel(u_ref, t_ref, m_ref, p_ref, alr_ref, ali_ref, y_ref, loc, sp, sr_s, si_s, *, steps, bsz):
    ns = S5_HG * S5_STATE

    @pl.when(pl.program_id(1) == 0)
    def _():
        sr_s[...] = jnp.zeros_like(sr_s)
        si_s[...] = jnp.zeros_like(si_s)

    nt = ns // LANE
    u = jnp.concatenate([u_ref[:, pl.ds(l, steps, stride=S5_L), :].reshape(bsz * steps, LANE)
                         for l in range(S5_L)], axis=1).astype(BF16)
    loc_v = _dot(u, m_ref[0])
    for j in range(2 * nt):
        loc[j] = loc_v[:, j * LANE:(j + 1) * LANE]
    ar = [jnp.broadcast_to(alr_ref[0, :, j * LANE:(j + 1) * LANE], (bsz, LANE)) for j in range(nt)]
    ai = [jnp.broadcast_to(ali_ref[0, :, j * LANE:(j + 1) * LANE], (bsz, LANE)) for j in range(nt)]

    def step(k, carry):
        sr, si = carry
        rows = pl.ds(k, bsz, stride=steps)
        nr, ni = [], []
        for j in range(nt):
            sp[j, rows, :] = sr[j]
            sp[nt + j, rows, :] = si[j]
            nr.append(ar[j] * sr[j] - ai[j] * si[j] + loc[j, rows, :])
            ni.append(ar[j] * si[j] + ai[j] * sr[j] + loc[nt + j, rows, :])
        return tuple(nr), tuple(ni)

    init = (tuple(sr_s[:, j * LANE:(j + 1) * LANE] for j in range(nt)),
            tuple(si_s[:, j * LANE:(j + 1) * LANE] for j in range(nt)))
    sr, si = lax.fori_loop(0, steps, step, init, unroll=4)
    for j in range(nt):
        sr_s[:, j * LANE:(j + 1) * LANE] = sr[j]
        si_s[:, j * LANE:(j + 1) * LANE] = si[j]
    s_prev = jnp.concatenate([sp[j] for j in range(2 * nt)], axis=1).astype(BF16)
    y = _dot(u, t_ref[0]) + _dot(s_prev, p_ref[0])
    for l in range(S5_L):
        y_ref[:, pl.ds(l, steps, stride=S5_L), :] = y[:, l * LANE:(l + 1) * LANE].reshape(bsz, steps, LANE)


def _s5_scan(s5u, mats, layer, bsz, seq, steps=64):
    n_chunks = seq // S5_L
    w = S5_L * LANE
    ns = S5_HG * S5_STATE
    nh = GROUP_W // LANE
    t_mat, m_mat, p_mat, al_r, al_i = mats

    def hspec(a, b):
        return pl.BlockSpec((None, 1, a, b), lambda hf, j: (layer, hf, 0, 0))

    rspec = pl.BlockSpec((bsz, steps * S5_L, LANE), lambda hf, j: (0, j, hf))
    y = pl.pallas_call(
        functools.partial(_s5_kernel, steps=steps, bsz=bsz),
        grid=(nh, n_chunks // steps),
        in_specs=[rspec, hspec(w, w), hspec(w, 2 * ns), hspec(2 * ns, w), hspec(1, ns), hspec(1, ns)],
        out_specs=rspec,
        out_shape=jax.ShapeDtypeStruct((bsz, seq, GROUP_W), F32),
        scratch_shapes=[pltpu.VMEM((2 * ns // LANE, bsz * steps, LANE), F32),
                        pltpu.VMEM((2 * ns // LANE, bsz * steps, LANE), F32),
                        pltpu.VMEM((bsz, ns), F32), pltpu.VMEM((bsz, ns), F32)],
        compiler_params=_cparams(("parallel", "arbitrary")),
        name="s5_scan",
    )(s5u.reshape(bsz, seq, GROUP_W), t_mat, m_mat, p_mat, al_r, al_i)
    return y.reshape(bsz * seq, GROUP_W)


def _moba_kernel(q_ref, k_ref, v_ref, qg_ref, kg_ref, o_ref, kx, vt, kmean, s_a, s_b, acc_s, m_s, *, n_blocks):
    i = pl.program_id(1)
    blk = MOBA_BLOCK
    hd = HEAD_DIM
    seq = n_blocks * blk

    @pl.when(i == 0)
    def _():
        kg = kg_ref[...]
        lane = lax.broadcasted_iota(jnp.int32, (seq, hd), 1)
        kblk = lax.broadcasted_iota(jnp.int32, (seq, hd), 0) // blk
        onehot = jnp.where(lane == kblk, 1.0, 0.0).astype(BF16)
        for h in range(N_HEADS):
            kh = _rms(k_ref[:, h * hd:(h + 1) * hd], kg)
            kx[h, :, 0:hd] = kh.astype(BF16)
            kx[h, :, hd:2 * hd] = onehot
            kmean[h] = jnp.mean(kh.reshape(n_blocks, blk, hd), axis=1)
        ones_rows = jnp.where(lax.broadcasted_iota(jnp.int32, (MOBA_VROWS - hd, blk), 0) == 0, 1.0, 0.0).astype(BF16)
        for nb in range(n_blocks):
            v_t = v_ref[nb * blk:(nb + 1) * blk, :].T
            for h in range(N_HEADS):
                r0 = h * MOBA_VROWS
                vt[r0:r0 + hd, nb * blk:(nb + 1) * blk] = v_t[h * hd:(h + 1) * hd, :].astype(BF16)
                vt[r0 + hd:r0 + MOBA_VROWS, nb * blk:(nb + 1) * blk] = ones_rows

    q_t = q_ref[...].T
    qg = qg_ref[...]
    key_ix = lax.broadcasted_iota(jnp.int32, (blk, blk), 0)
    qry_ix = lax.broadcasted_iota(jnp.int32, (blk, blk), 1)
    causal = key_ix <= qry_ix
    bid = lax.broadcasted_iota(jnp.int32, (n_blocks, blk), 0)
    past = bid < i
    own0 = pl.multiple_of(i * blk, blk)

    qns, qss, s_own = [], [], []
    for h in range(N_HEADS):
        qh = q_t[h * hd:(h + 1) * hd, :]
        qn = qh * lax.rsqrt(jnp.mean(qh * qh, axis=0, keepdims=True) + EPS) * qg
        qs = qn * (hd ** -0.5 * LOG2E)
        qns.append(qn)
        qss.append(qs)
        s_own.append(_dot(kx[h, pl.ds(own0, blk), :],
                          jnp.concatenate([qs, jnp.zeros((hd, blk), F32)], axis=0).astype(BF16)))

    qxs = []
    for h in range(N_HEADS):
        qn = qns[h]
        gate = jnp.dot(kmean[h], qn, preferred_element_type=F32, precision=HIGHEST)
        gate = jnp.where(past, gate, -jnp.inf)
        cnt = jnp.zeros((n_blocks, blk), F32)
        for m in range(n_blocks):
            gm = gate[m:m + 1, :]
            cnt = cnt + jnp.where(gm > gate, 1.0, jnp.where((gm == gate) & (bid > m), 1.0, 0.0))
        bias = jnp.where((bid == i) | (past & (cnt < MOBA_TOPK)), 0.0, NEG)
        qxs.append(jnp.concatenate([qss[h], bias, jnp.zeros((hd - n_blocks, blk), F32)],
                                   axis=0).astype(BF16))

    heads = range(N_HEADS)

    def issue_scores(n, dst):
        k0 = pl.multiple_of(n * blk, blk)
        for h in heads:
            dst[h] = _dot(kx[h, pl.ds(k0, blk), :], qxs[h])

    def values(n, ps):
        k0 = pl.multiple_of(n * blk, blk)
        return [_dot(vt[h * MOBA_VROWS:(h + 1) * MOBA_VROWS, pl.ds(k0, blk)], ps[h]) for h in heads]

    ps = []
    for h in heads:
        s = jnp.where(causal, s_own[h], NEG)
        m0 = jnp.max(s, axis=0, keepdims=True)
        p = jnp.exp2(s - m0)
        m_s[h] = m0
        ps.append(p.astype(BF16))
    for h, pv in zip(heads, values(i, ps)):
        acc_s[h] = pv

    def step(n, src, dst):
        issue_scores(jnp.minimum(n + 1, jnp.maximum(i - 1, 0)), dst)
        ps, alphas = [], []
        for h in heads:
            m_run = m_s[h]
            s = src[h]
            m_new = jnp.maximum(m_run, jnp.max(s, axis=0, keepdims=True))
            alpha = jnp.exp2(m_run - m_new)
            p = jnp.exp2(s - m_new)
            m_s[h] = m_new
            alphas.append(alpha)
            ps.append(p.astype(BF16))
        for h, pv in zip(heads, values(n, ps)):
            acc_s[h] = alphas[h] * acc_s[h] + pv

    issue_scores(0, s_a)

    def pair(k, carry):
        step(2 * k, s_a, s_b)
        step(2 * k + 1, s_b, s_a)
        return carry

    lax.fori_loop(0, i // 2, pair, 0)

    @pl.when(i % 2 == 1)
    def _():
        step(i - 1, s_a, s_b)

    o_ref[...] = jnp.concatenate([acc_s[h, 0:hd, :] / acc_s[h, hd:hd + 1, :] for h in heads],
                                 axis=0).T


def _moba(qkv, q_g, k_g, bsz, seq):
    n_blocks = seq // MOBA_BLOCK
    n = bsz * seq
    return pl.pallas_call(
        functools.partial(_moba_kernel, n_blocks=n_blocks),
        grid=(bsz, n_blocks),
        in_specs=[pl.BlockSpec((MOBA_BLOCK, GROUP_W), lambda b, i: (b * n_blocks + i, 0)),
                  pl.BlockSpec((seq, GROUP_W), lambda b, i: (b, 1)),
                  pl.BlockSpec((seq, GROUP_W), lambda b, i: (b, 2)),
                  pl.BlockSpec((HEAD_DIM, 1), lambda b, i: (0, 0)),
                  pl.BlockSpec((1, HEAD_DIM), lambda b, i: (0, 0))],
        out_specs=pl.BlockSpec((MOBA_BLOCK, GROUP_W), lambda b, i: (b * n_blocks + i, 0)),
        out_shape=jax.ShapeDtypeStruct((n, GROUP_W), F32),
        scratch_shapes=[pltpu.VMEM((N_HEADS, seq, 2 * HEAD_DIM), BF16),
                        pltpu.VMEM((N_HEADS * MOBA_VROWS, seq), BF16),
                        pltpu.VMEM((N_HEADS, n_blocks, HEAD_DIM), F32),
                        pltpu.VMEM((N_HEADS, MOBA_BLOCK, MOBA_BLOCK), F32),
                        pltpu.VMEM((N_HEADS, MOBA_BLOCK, MOBA_BLOCK), F32),
                        pltpu.VMEM((N_HEADS, MOBA_VROWS, MOBA_BLOCK), F32),
                        pltpu.VMEM((N_HEADS, 1, MOBA_BLOCK), F32)],
        compiler_params=_cparams(("parallel", "arbitrary")),
        name="moba",
    )(qkv, qkv, qkv, q_g.reshape(HEAD_DIM, 1), k_g.reshape(1, HEAD_DIM))


def _ssd_kernel(zs_ref, xbc_ref, dt_ref, dtb_ref, a_ref, d_ref, o_ref, state):
    c = pl.program_id(1)
    lc = SSD_CHUNK
    gn = SSD_STATE
    hd = HEAD_DIM
    hpg = N_HEADS // SSD_GROUPS
    chunks = range(SSD_STEP_CHUNKS)
    heads = range(N_HEADS)

    @pl.when(c == 0)
    def _():
        state[...] = jnp.zeros_like(state)

    xbc_all = xbc_ref[...]
    dtr = dt_ref[...] + dtb_ref[...]
    dt_all = jnp.maximum(dtr, 0.0) + jnp.log(1.0 + jnp.exp(-jnp.abs(dtr)))
    adt_all = dt_all * a_ref[...]
    row = lax.broadcasted_iota(jnp.int32, (lc, lc), 0)
    col = lax.broadcasted_iota(jnp.int32, (lc, lc), 1)
    causal = col <= row
    tril = jnp.where(causal, 1.0, 0.0)
    zs_all = zs_ref[...]
    dd = d_ref[...]
    rows = [slice(cc * lc, (cc + 1) * lc) for cc in chunks]
    xbc = [xbc_all[r] for r in rows]
    bmat = [[x[:, GROUP_W + g * gn:GROUP_W + (g + 1) * gn] for g in range(SSD_GROUPS)] for x in xbc]
    cmat = [[x[:, GROUP_W + (SSD_GROUPS + g) * gn:GROUP_W + (SSD_GROUPS + g + 1) * gn].astype(BF16)
             for g in range(SSD_GROUPS)] for x in xbc]
    s_in = [state[h] for h in heads]
    y_off = [[_dot(cmat[0][h // hpg], s_in[h].astype(BF16)) for h in heads]]
    cb = [[_dot_nt(cmat[cc][g], bmat[cc][g].astype(BF16)) for g in range(SSD_GROUPS)] for cc in chunks]
    cs = [jnp.dot(tril, adt_all[r], preferred_element_type=F32, precision=HIGHEST) for r in rows]
    cs_t = [x.T for x in cs]
    y_diag, upd, cs_b, x_h = [], [], [], []
    for cc in chunks:
        yd_c, upd_c, csb_c, xh_c = [], [], [], []
        for h in heads:
            g = h // hpg
            csb = jnp.broadcast_to(cs[cc][:, h:h + 1], (lc, lc))
            seg = csb - cs_t[cc][h:h + 1, :]
            decay = jnp.where(causal, jnp.exp(jnp.where(causal, seg, 0.0)), 0.0)
            scores = (cb[cc][g] * decay).astype(BF16)
            xh = xbc[cc][:, h * hd:(h + 1) * hd]
            xd = (xh * jnp.broadcast_to(dt_all[rows[cc]][:, h:h + 1], (lc, hd))).astype(BF16)
            bd = bmat[cc][g] * jnp.exp(csb[lc - 1:lc, :] - csb)
            yd_c.append(_dot(scores, xd))
            upd_c.append(lax.dot_general(bd.astype(BF16), xd, (((0,), (0,)), ((), ())), preferred_element_type=F32))
            csb_c.append(csb)
            xh_c.append(xh)
        y_diag.append(yd_c); upd.append(upd_c); cs_b.append(csb_c); x_h.append(xh_c)
    s_run = s_in
    for cc in chunks:
        if cc > 0:
            y_off.append([_dot(cmat[cc][h // hpg], s_run[h].astype(BF16)) for h in heads])
        s_run = [jnp.exp(cs_b[cc][h][lc - 1:lc, :hd]) * s_run[h] + upd[cc][h] for h in heads]
    for h in heads:
        state[h] = s_run[h]
    for cc in chunks:
        for h in heads:
            hs = slice(h * hd, (h + 1) * hd)
            y = y_diag[cc][h] + y_off[cc][h] * jnp.exp(cs_b[cc][h][:, :hd]) + dd[:, hs] * x_h[cc][h]
            o_ref[rows[cc], hs] = y * zs_all[rows[cc], hs]


def _ssd(zs, xbc, dt, dt_bias, a_log, d_skip, bsz, seq):
    n = bsz * seq
    rows = SSD_STEP_CHUNKS * SSD_CHUNK
    nc = seq // rows
    dtb = jnp.zeros((1, LANE), F32).at[0, :N_HEADS].set(dt_bias)
    a = jnp.zeros((1, LANE), F32).at[0, :N_HEADS].set(-jnp.exp(a_log))
    dfull = jnp.repeat(d_skip, HEAD_DIM)[None, :]

    def const(shape):
        return pl.BlockSpec(shape, lambda b, c: (0, 0))

    return pl.pallas_call(
        _ssd_kernel,
        grid=(bsz, nc),
        in_specs=[pl.BlockSpec((rows, GROUP_W), lambda b, c: (b * nc + c, 0)),
                  pl.BlockSpec((rows, SSD_CONV_CH), lambda b, c: (b * nc + c, 0)),
                  pl.BlockSpec((rows, LANE), lambda b, c: (b * nc + c, 0)),
                  const((1, LANE)), const((1, LANE)), const((1, GROUP_W))],
        out_specs=pl.BlockSpec((rows, GROUP_W), lambda b, c: (b * nc + c, 0)),
        out_shape=jax.ShapeDtypeStruct((n, GROUP_W), F32),
        scratch_shapes=[pltpu.VMEM((N_HEADS, SSD_STATE, HEAD_DIM), F32)],
        compiler_params=_cparams(("parallel", "arbitrary")),
        name="ssd",
    )(zs, xbc, dt, dtb, a, dfull)


def _gelu_tanh(x):
    return 0.5 * x * (1.0 + jnp.tanh(math.sqrt(2.0 / math.pi) * (x + 0.044715 * (x * x * x))))


def _group_norm(y, g):
    return (y * lax.rsqrt(jnp.mean(y * y, axis=-1, keepdims=True) + EPS) * g).astype(BF16)


def _merge_rows(h, ys5, s5u, att, yc, ssd, s5d, wglu_ref, mg, wo_ref):
    gw = GROUP_W
    ya = ys5 + s5d * s5u
    ga = _gelu_tanh(ya)
    ya = ga * _sigmoid(_dot(ga.astype(BF16), wglu_ref[...]))
    acc = h
    for j, y in enumerate((ya, att, yc, ssd)):
        acc = acc + _dot(_group_norm(y, mg[:, j * gw:(j + 1) * gw]), wo_ref[j * gw:(j + 1) * gw, :])
    return acc


def _tail_kernel(*refs, tm, seq):
    main = refs[0:6]
    (p_ref, s5d_ref, wglu_ref, mg_ref, wo_ref, g_ref, wg_ref, wu_ref, cw_ref, wd_ref,
     pg_ref, pwg_ref, pwp_ref, o_ref, gh_s) = refs[6:]
    first = (pl.program_id(0) * tm) % seq == 0

    @pl.when(pl.program_id(0) == 0)
    def _():
        gh_s[...] = jnp.zeros_like(gh_s)

    s5d = s5d_ref[...]
    mg = mg_ref[...]
    rm = tm // ROW_SPLIT
    rows = [slice(a * rm, (a + 1) * rm) for a in range(ROW_SPLIT)]
    hs = [_merge_rows(*[r[rw, :] for r in main], s5d, wglu_ref, mg, wo_ref) for rw in rows]
    g = g_ref[...]
    vs = [_rms(h, g).astype(BF16) for h in hs]
    ffn = [None] * ROW_SPLIT
    splits = [slice(c0, c1) for c0, c1 in zip(FFN_SPLITS[:-1], FFN_SPLITS[1:])]

    def gate_proj(cs):
        gates = [_dot(v, wg_ref[:, cs]) for v in vs]
        prev = jnp.where(first, 0.0, gh_s[:, cs])
        gh_s[:, cs] = gates[-1][rm - HALO:, :]
        return gates, prev

    nxt = gate_proj(splits[0])
    for ci, cs in enumerate(splits):
        gates, gate_h = nxt
        ups = [_dot(v, wu_ref[:, cs]) for v in vs]
        if ci + 1 < len(splits):
            nxt = gate_proj(splits[ci + 1])
        acts = []
        for a in range(ROW_SPLIT):
            conv = _causal_conv(gates[a], gate_h, cw_ref, FFN_CONV_K, cs)
            acts.append((_silu(conv) * ups[a]).astype(BF16))
            gate_h = gates[a][rm - HALO:, :]
        for a in range(ROW_SPLIT):
            down = _dot(acts[a], wd_ref[cs, :])
            ffn[a] = down if ffn[a] is None else ffn[a] + down
    for a, rw in enumerate(rows):
        acc = hs[a] + ffn[a]
        gate = _sigmoid(_dot(_rms(acc, pg_ref[...]).astype(BF16), pwg_ref[...]))
        o_ref[rw, :] = acc + gate * _dot(p_ref[rw, :].astype(BF16), pwp_ref[...])


def _tail(h, ys5, s5u, att, yc, ssd, p, layer, s5_d, w_glu, merge_g, w_out, ffn_g, w_gate, w_up, conv_w, w_down,
          ple_g, ple_w_gate, ple_w_proj, seq, tm=512):
    n = h.shape[0]
    per = tm // HALO
    widths = (D_MODEL,) + (GROUP_W,) * 5

    def resident(shape):
        return pl.BlockSpec(shape, lambda i: (0, 0), pipeline_mode=pl.Buffered(1))

    def stacked(shape):
        return pl.BlockSpec((None,) + shape, lambda i: (layer, 0, 0), pipeline_mode=pl.Buffered(1))

    return pl.pallas_call(
        functools.partial(_tail_kernel, tm=tm, seq=seq),
        grid=(n // tm,),
        in_specs=([pl.BlockSpec((tm, wd), lambda i: (i, 0)) for wd in widths]
                  + [pl.BlockSpec((None, tm, PLE_DIM), lambda i: (layer, i, 0)),
                     resident((1, GROUP_W)), stacked((GROUP_W, GROUP_W)), resident((1, D_MODEL)),
                     stacked((D_MODEL, D_MODEL)), resident((1, D_MODEL)), stacked((D_MODEL, D_FF)),
                     stacked((D_MODEL, D_FF)), resident((FFN_CONV_K, D_FF)), stacked((D_FF, D_MODEL)),
                     resident((1, D_MODEL)), stacked((D_MODEL, D_MODEL)), stacked((PLE_DIM, D_MODEL))]),
        out_specs=pl.BlockSpec((tm, D_MODEL), lambda i: (i, 0)),
        out_shape=jax.ShapeDtypeStruct((n, D_MODEL), F32),
        scratch_shapes=[pltpu.VMEM((HALO, D_FF), F32)],
        compiler_params=_cparams(("arbitrary",)),
        name="tail",
    )(h, ys5, s5u, att, yc, ssd, p,
      s5_d[None, :], w_glu, merge_g[None, :], w_out, ffn_g[None, :], w_gate, w_up, conv_w, w_down,
      ple_g[None, :], ple_w_gate, ple_w_proj)


def kernel(x, p, mix_norm_g, w_in, s5_a_re, s5_a_im, s5_log_dt, s5_b_re, s5_b_im, s5_c_re, s5_c_im, s5_d, s5_w_glu, moba_q_g, moba_k_g, sconv_w, ssd_conv_w, ssd_conv_b, ssd_dt_bias, ssd_a_log, ssd_d, merge_norm_g, w_out, ffn_norm_g, ffn_w_gate, ffn_w_up, ffn_conv_w, ffn_w_down, ple_norm_g, ple_w_gate, ple_w_proj):
    bsz, seq, _ = x.shape
    depth = w_in.shape[0]
    n = bsz * seq
    h = x.reshape(n, D_MODEL)
    w_in_b = jnp.pad(w_in, ((0, 0), (0, 0), (0, PROJ_PAD - w_in.shape[2]))).astype(BF16)
    w_glu_b, w_out_b = s5_w_glu.astype(BF16), w_out.astype(BF16)
    w_gate_b, w_up_b, w_down_b = ffn_w_gate.astype(BF16), ffn_w_up.astype(BF16), ffn_w_down.astype(BF16)
    pw_gate_b, pw_proj_b = ple_w_gate.astype(BF16), ple_w_proj.astype(BF16)
    s5_mats = jax.vmap(_s5_param_mats)(s5_a_re, s5_a_im, s5_log_dt, s5_b_re, s5_b_im, s5_c_re, s5_c_im)
    p_rows = p.reshape(depth, n, PLE_DIM)
    for i in range(depth):
        s5u, qkv, yc, zs, xbc, dt = _inproj(h, mix_norm_g[i][None, :], w_in_b, i, sconv_w[i],
                                            ssd_conv_w[i], ssd_conv_b[i], seq)
        ys5 = _s5_scan(s5u, s5_mats, i, bsz, seq)
        att = _moba(qkv, moba_q_g[i][None, :], moba_k_g[i][None, :], bsz, seq)
        yd = _ssd(zs, xbc, dt, ssd_dt_bias[i], ssd_a_log[i], ssd_d[i], bsz, seq)
        h = _tail(h, ys5, s5u, att, yc, yd, p_rows, i, s5_d[i], w_glu_b, merge_norm_g[i], w_out_b,
                  ffn_norm_g[i], w_gate_b, w_up_b, ffn_conv_w[i], w_down_b,
                  ple_norm_g[i], pw_gate_b, pw_proj_b, seq)
    return h.reshape(bsz, seq, D_MODEL)
```
